```python
import jax, jax.numpy as jnp
from jax import lax
import numpy as np

D_MODEL = 4096
BATCH = 1
SEQ = 16384
DEPTH = 1
DEC_BATCH = 2
DEC_SEQ = 4096
PAST_LEN = 128

HEAD_DIM = 128
RET_QK_DIM = 128
RET_V_DIM = 256
RET_V_W = D_MODEL // 2
ATTN_W = D_MODEL // 2
MIX_W = RET_V_W + ATTN_W
N_RET_HEADS = RET_V_W // RET_V_DIM
RET_Q_W = N_RET_HEADS * RET_QK_DIM
N_ATTN_HEADS = ATTN_W // HEAD_DIM
RET_CHUNK = 128
DIL_BRANCHES = ((128, 1), (512, 4), (2048, 16))
N_EXPERTS = 16
EXPERT_CAPACITY = 2
D_EXPERT = 11008
RMS_EPS = 1e-6
IN_SPLITS = (RET_Q_W, RET_Q_W, RET_V_W, RET_V_W, ATTN_W, ATTN_W, ATTN_W)
IN_W = RET_Q_W * 2 + RET_V_W * 2 + ATTN_W * 3

kernel_name = "hybrid_retention_dilated_attn_ec_moe_encoder"


def rmsnorm(x, g):
    xf = x.astype(jnp.float32)
    y = xf * lax.rsqrt(jnp.mean(xf * xf, axis=-1, keepdims=True) + RMS_EPS)
    return (y * g.astype(jnp.float32)).astype(x.dtype)


def retention_one_direction(q, k, v, log_gamma, include_diag):
    b, t, h, dk = q.shape
    dv = v.shape[-1]
    c = RET_CHUNK
    nc = t // c
    qc = q.reshape(b, nc, c, h, dk)
    kc = k.reshape(b, nc, c, h, dk)
    vc = v.reshape(b, nc, c, h, dv)
    pos = jnp.arange(c, dtype=jnp.float32)
    diff = pos[:, None] - pos[None, :]
    keep = (diff >= 0) if include_diag else (diff > 0)
    decay = jnp.where(keep[None], jnp.exp(jnp.maximum(diff, 0.0)[None] * log_gamma[:, None, None]), 0.0).astype(q.dtype)
    scores = jnp.einsum('bnahd,bnchd->bnhac', qc, kc) * decay
    inner = jnp.einsum('bnhac,bnche->bnahe', scores, vc)
    k_w = jnp.exp((c - 1 - pos)[:, None] * log_gamma[None, :]).astype(q.dtype)
    chunk_kv = jnp.einsum('bnchd,ch,bnche->bnhde', kc, k_w, vc)
    g_chunk = jnp.exp(c * log_gamma).astype(q.dtype)[None, :, None, None]

    def step(state, kv):
        return state * g_chunk + kv, state

    _, prev = lax.scan(step, jnp.zeros((b, h, dk, dv), chunk_kv.dtype), jnp.moveaxis(chunk_kv, 1, 0))
    prev = jnp.moveaxis(prev, 0, 1)
    q_w = jnp.exp((pos + 1.0)[:, None] * log_gamma[None, :]).astype(q.dtype)
    cross = jnp.einsum('bnahd,bnhde->bnahe', qc, prev) * q_w[None, None, :, :, None]
    return (inner + cross).reshape(b, t, h, dv)


def bidirectional_retention(q, k, v, decay_fwd, decay_bwd):
    lg_f = jnp.log1p(-jnp.exp2(-decay_fwd.astype(jnp.float32)))
    lg_b = jnp.log1p(-jnp.exp2(-decay_bwd.astype(jnp.float32)))
    fwd = retention_one_direction(q, k, v, lg_f, True)
    bwd = jnp.flip(retention_one_direction(jnp.flip(q, 1), jnp.flip(k, 1), jnp.flip(v, 1), lg_b, False), 1)
    return fwd + bwd


def alibi_slopes(n_heads):
    return jnp.exp2(-8.0 * jnp.arange(1, n_heads + 1, dtype=jnp.float32) / n_heads)


def dilated_branch(q, k, v, slopes, window, dilation):
    b, t, h, dh = q.shape
    radius = window // (2 * dilation)
    blk = radius
    n_sub = t // dilation
    nb = -(-n_sub // blk)
    lp = nb * blk
    n = b * dilation

    def to_sub(z):
        return z.reshape(b, n_sub, dilation, h, dh).transpose(0, 2, 1, 3, 4).reshape(n, n_sub, h, dh)

    def key_blocks(z):
        zp = jnp.pad(z, ((0, 0), (blk, lp - n_sub + blk), (0, 0), (0, 0))).reshape(n, nb + 2, blk, h, dh)
        return jnp.concatenate([zp[:, :-2], zp[:, 1:-1], zp[:, 2:]], axis=2)

    qb = jnp.pad(to_sub(q), ((0, 0), (0, lp - n_sub), (0, 0), (0, 0))).reshape(n, nb, blk, h, dh)
    kb = key_blocks(to_sub(k))
    vb = key_blocks(to_sub(v))
    blk_start = jnp.arange(nb)[:, None] * blk
    qpos = blk_start + jnp.arange(blk)[None, :]
    kpos = blk_start - blk + jnp.arange(3 * blk)[None, :]
    dist = jnp.abs(qpos[:, :, None] - kpos[:, None, :])
    valid = (dist <= radius) & (kpos >= 0)[:, None, :] & (kpos < n_sub)[:, None, :]
    bias = -slopes[None, :, None, None] * (dilation * dist).astype(jnp.float32)[:, None]
    scores = jnp.einsum('nbqhd,nbkhd->nbhqk', qb, kb).astype(jnp.float32) + bias
    scores = jnp.where(valid[:, None], scores, -jnp.inf)
    lse = jax.nn.logsumexp(scores, axis=-1)
    probs = jnp.exp(scores - lse[..., None]).astype(v.dtype)
    o = jnp.einsum('nbhqk,nbkhd->nbqhd', probs, vb).reshape(n, lp, h, dh)[:, :n_sub]
    lse = lse.transpose(0, 1, 3, 2).reshape(n, lp, h)[:, :n_sub]
    o = o.reshape(b, dilation, n_sub, h, dh).transpose(0, 2, 1, 3, 4).reshape(b, t, h, dh)
    lse = lse.reshape(b, dilation, n_sub, h).transpose(0, 2, 1, 3).reshape(b, t, h)
    return o, lse


def dilated_attention(q, k, v):
    slopes = alibi_slopes(q.shape[2])
    outs, lses = [], []
    for window, dilation in DIL_BRANCHES:
        o_i, l_i = dilated_branch(q, k, v, slopes, window, dilation)
        outs.append(o_i)
        lses.append(l_i)
    o = jnp.stack(outs)
    w = jax.nn.softmax(jnp.stack(lses), axis=0).astype(o.dtype)
    return jnp.sum(w[..., None] * o, axis=0)


def hybrid_mixer(x, norm_g, w_in, decay_fwd, decay_bwd, ret_norm_g, q_norm_g, k_norm_g, attn_out_g, w_out):
    b, t, _ = x.shape
    h = rmsnorm(x, norm_g)
    proj = jnp.einsum('btd,de->bte', h, w_in)
    split_points = np.cumsum(IN_SPLITS)[:-1].tolist()
    rq, rk, rv, rg, aq, ak, av = jnp.split(proj, split_points, axis=-1)
    rq = rq.reshape(b, t, N_RET_HEADS, RET_QK_DIM)
    rk = rk.reshape(b, t, N_RET_HEADS, RET_QK_DIM) * (RET_QK_DIM ** -0.5)
    rv = rv.reshape(b, t, N_RET_HEADS, RET_V_DIM)
    ret = bidirectional_retention(rq, rk, rv, decay_fwd, decay_bwd)
    ret = rmsnorm(ret, ret_norm_g.reshape(N_RET_HEADS, RET_V_DIM)).reshape(b, t, RET_V_W)
    ret = ret * jax.nn.silu(rg)
    aq = rmsnorm(aq.reshape(b, t, N_ATTN_HEADS, HEAD_DIM), q_norm_g) * (HEAD_DIM ** -0.5)
    ak = rmsnorm(ak.reshape(b, t, N_ATTN_HEADS, HEAD_DIM), k_norm_g)
    av = av.reshape(b, t, N_ATTN_HEADS, HEAD_DIM)
    att = dilated_attention(aq, ak, av).reshape(b, t, ATTN_W)
    att = rmsnorm(att, attn_out_g)
    return jnp.einsum('bte,ed->btd', jnp.concatenate([ret, att], axis=-1), w_out)


def expert_choice_ffn(x, norm_g, w_router, w_gate, w_up, w_down):
    b, t, d = x.shape
    n = b * t
    cap = EXPERT_CAPACITY * n // N_EXPERTS
    h = rmsnorm(x, norm_g).reshape(n, d)
    affinity = jax.nn.softmax(jnp.einsum('nd,de->ne', h, w_router).astype(jnp.float32), axis=-1)
    gates, idx = lax.top_k(affinity.T, cap)
    xe = h[idx]
    hid = jax.nn.silu(jnp.einsum('ecd,edf->ecf', xe, w_gate)) * jnp.einsum('ecd,edf->ecf', xe, w_up)
    ye = jnp.einsum('ecf,efd->ecd', hid, w_down) * gates[..., None].astype(x.dtype)
    out = jnp.zeros((n, d), x.dtype).at[idx.reshape(-1)].add(ye.reshape(-1, d))
    return out.reshape(b, t, d)


def encoder_layer(x, attn_norm_g, w_in, ret_decay_fwd, ret_decay_bwd, ret_norm_g, q_norm_g, k_norm_g,
                  attn_out_g, w_out, ffn_norm_g, w_router, w_gate, w_up, w_down):
    x = x + hybrid_mixer(x, attn_norm_g, w_in, ret_decay_fwd, ret_decay_bwd, ret_norm_g, q_norm_g, k_norm_g, attn_out_g, w_out)
    x = x + expert_choice_ffn(x, ffn_norm_g, w_router, w_gate, w_up, w_down)
    return x


def setup_inputs(seed: int = 0) -> dict:
    key = jax.random.key(seed)
    ks = jax.random.split(key, 17)
    f32 = jnp.float32

    def gain(k, shape):
        return 1.0 + 0.02 * jax.random.normal(k, shape, f32)

    base_decay = 5.0 + jnp.arange(N_RET_HEADS, dtype=f32)
    return {
        "x_prompt": jax.random.normal(ks[0], (BATCH, SEQ, D_MODEL), f32),
        "x_sample": jax.random.normal(ks[1], (DEC_BATCH, DEC_SEQ, D_MODEL), f32),
        "attn_norm_g": gain(ks[2], (DEPTH, D_MODEL)),
        "w_in": jax.random.normal(ks[3], (DEPTH, D_MODEL, IN_W), f32) * D_MODEL ** -0.5,
        "ret_decay_fwd": base_decay + 0.1 * jax.random.normal(ks[4], (DEPTH, N_RET_HEADS), f32),
        "ret_decay_bwd": base_decay + 0.1 * jax.random.normal(ks[5], (DEPTH, N_RET_HEADS), f32),
        "ret_norm_g": gain(ks[6], (DEPTH, RET_V_W)),
        "q_norm_g": gain(ks[7], (DEPTH, HEAD_DIM)),
        "k_norm_g": gain(ks[8], (DEPTH, HEAD_DIM)),
        "attn_out_g": gain(ks[9], (DEPTH, ATTN_W)),
        "w_out": jax.random.normal(ks[10], (DEPTH, MIX_W, D_MODEL), f32) * MIX_W ** -0.5,
        "ffn_norm_g": gain(ks[11], (DEPTH, D_MODEL)),
        "w_router": jax.random.normal(ks[12], (DEPTH, D_MODEL, N_EXPERTS), f32) * D_MODEL ** -0.5,
        "w_gate": jax.random.normal(ks[13], (DEPTH, N_EXPERTS, D_MODEL, D_EXPERT), f32) * D_MODEL ** -0.5,
        "w_up": jax.random.normal(ks[14], (DEPTH, N_EXPERTS, D_MODEL, D_EXPERT), f32) * D_MODEL ** -0.5,
        "w_down": jax.random.normal(ks[15], (DEPTH, N_EXPERTS, D_EXPERT, D_MODEL), f32) * D_EXPERT ** -0.5,
    }


def reference(x_prompt, x_sample, attn_norm_g, w_in, ret_decay_fwd, ret_decay_bwd, ret_norm_g, q_norm_g,
              k_norm_g, attn_out_g, w_out, ffn_norm_g, w_router, w_gate, w_up, w_down):
    y_prompt = x_prompt
    y_sample = x_sample
    for l in range(DEPTH):
        params = (attn_norm_g[l], w_in[l], ret_decay_fwd[l], ret_decay_bwd[l], ret_norm_g[l], q_norm_g[l],
                  k_norm_g[l], attn_out_g[l], w_out[l], ffn_norm_g[l], w_router[l], w_gate[l], w_up[l], w_down[l])
        y_prompt = encoder_layer(y_prompt, *params)
        y_sample = encoder_layer(y_sample, *params)
    return (y_prompt, y_sample)
```

```python
import functools
import math

import jax
import jax.numpy as jnp
from jax import lax
from jax.experimental import pallas as pl
from jax.experimental.pallas import tpu as pltpu

F32 = jnp.float32
BF16 = jnp.bfloat16
I32 = jnp.int32

RMS_EPS = 1e-6
RET_QK_DIM = 128
RET_V_DIM = 256
HEAD_DIM = 128
RET_CHUNK = 128
DIL_BRANCHES = ((128, 1), (512, 4), (2048, 16))
DIL_RADIUS = 64
EXPERT_CAPACITY = 2
LANES = 128
MASKED = -1e30
MOE_OUT_CHUNK = 512

V7X_VMEM_BYTES = 64 * 1024 * 1024


def _cparams(n_axes, vmem_mb):
    assert vmem_mb * 1024 * 1024 <= V7X_VMEM_BYTES
    return pltpu.CompilerParams(
        dimension_semantics=("arbitrary",) * n_axes,
        vmem_limit_bytes=vmem_mb * 1024 * 1024,
    )


def _any_of(idx, values):
    out = idx == values[0]
    for v in values[1:]:
        out = jnp.logical_or(out, idx == v)
    return out


def _dot(a, b):
    return jnp.dot(a, b, preferred_element_type=F32)


def _dot_nt(a, b):
    return lax.dot_general(a, b, (((1,), (1,)), ((), ())), preferred_element_type=F32)


def _rmsnorm_body(x_ref, g_ref, o_ref):
    x = x_ref[...]
    ms = jnp.mean(x * x, axis=-1, keepdims=True)
    o_ref[...] = (x * lax.rsqrt(ms + RMS_EPS) * g_ref[...]).astype(o_ref.dtype)


def _rmsnorm_rows(x, g, out_dtype, tm):
    n, d = x.shape
    return pl.pallas_call(
        _rmsnorm_body,
        grid=(n // tm,),
        in_specs=[pl.BlockSpec((tm, d), lambda i: (i, 0)), pl.BlockSpec((1, d), lambda i: (0, 0))],
        out_specs=pl.BlockSpec((tm, d), lambda i: (i, 0)),
        out_shape=jax.ShapeDtypeStruct((n, d), out_dtype),
        compiler_params=_cparams(1, 40),
        name="rmsnorm",
    )(x, g.reshape(1, d))


def _head_rmsnorm(x, g_row, scale):
    parts = []
    for c in range(x.shape[1] // HEAD_DIM):
        xs = x[:, c * HEAD_DIM:(c + 1) * HEAD_DIM]
        ms = jnp.mean(xs * xs, axis=-1, keepdims=True)
        parts.append(xs * lax.rsqrt(ms + RMS_EPS) * g_row * scale)
    return parts


def _in_proj_body(h_ref, w_ref, gq_ref, gk_ref, o_ref, *, rk_tiles, aq_tiles, ak_tiles):
    j = pl.program_id(1)
    acc = _dot(h_ref[...], w_ref[...])
    is_rk = jnp.logical_and(j >= rk_tiles[0], j < rk_tiles[1])
    is_aq = jnp.logical_and(j >= aq_tiles[0], j < aq_tiles[1])
    is_ak = jnp.logical_and(j >= ak_tiles[0], j < ak_tiles[1])
    is_plain = jnp.logical_not(jnp.logical_or(is_rk, jnp.logical_or(is_aq, is_ak)))

    @pl.when(is_plain)
    def _():
        o_ref[...] = acc

    @pl.when(is_rk)
    def _():
        o_ref[...] = acc * (RET_QK_DIM ** -0.5)

    @pl.when(is_aq)
    def _():
        for c, part in enumerate(_head_rmsnorm(acc, gq_ref[...], HEAD_DIM ** -0.5)):
            o_ref[:, c * HEAD_DIM:(c + 1) * HEAD_DIM] = part

    @pl.when(is_ak)
    def _():
        for c, part in enumerate(_head_rmsnorm(acc, gk_ref[...], 1.0)):
            o_ref[:, c * HEAD_DIM:(c + 1) * HEAD_DIM] = part


def _in_proj(h, w, gq, gk, ret_q_w, ret_v_w, attn_w, tm, tn):
    n, d = h.shape
    in_w = w.shape[1]
    offs = (ret_q_w, 2 * ret_q_w + 2 * ret_v_w, 2 * ret_q_w + 2 * ret_v_w + attn_w)
    for o in offs + (ret_q_w, attn_w):
        assert o % tn == 0
    rk_tiles = (offs[0] // tn, (offs[0] + ret_q_w) // tn)
    aq_tiles = (offs[1] // tn, (offs[1] + attn_w) // tn)
    ak_tiles = (offs[2] // tn, (offs[2] + attn_w) // tn)
    body = functools.partial(_in_proj_body, rk_tiles=rk_tiles, aq_tiles=aq_tiles, ak_tiles=ak_tiles)
    return pl.pallas_call(
        body,
        grid=(n // tm, in_w // tn),
        in_specs=[
            pl.BlockSpec((tm, d), lambda i, j: (i, 0)),
            pl.BlockSpec((d, tn), lambda i, j: (0, j)),
            pl.BlockSpec((1, HEAD_DIM), lambda i, j: (0, 0)),
            pl.BlockSpec((1, HEAD_DIM), lambda i, j: (0, 0)),
        ],
        out_specs=pl.BlockSpec((tm, tn), lambda i, j: (i, j)),
        out_shape=jax.ShapeDtypeStruct((n, in_w), F32),
        compiler_params=_cparams(2, 56),
        name="in_proj",
    )(h, w, gq.reshape(1, HEAD_DIM), gk.reshape(1, HEAD_DIM))


def _ret_fwd_body(lgf_ref, lgb_ref, q_ref, k_ref, v_ref, o_ref, dmat_ref, state_ref, *, n_heads, start_chunks):
    c = pl.program_id(0)
    C = RET_CHUNK

    @pl.when(c == 0)
    def _():
        row = lax.broadcasted_iota(I32, (C, C), 0)
        col = lax.broadcasted_iota(I32, (C, C), 1)
        diff = (row - col).astype(F32)
        for h in range(n_heads):
            fwd = jnp.where(diff >= 0, jnp.exp(jnp.maximum(diff, 0.0) * lgf_ref[h]), 0.0)
            bwd = jnp.where(diff < 0, jnp.exp(jnp.maximum(-diff, 0.0) * lgb_ref[h]), 0.0)
            dmat_ref[h] = fwd + bwd

    @pl.when(_any_of(c, start_chunks))
    def _():
        state_ref[...] = jnp.zeros_like(state_ref)

    pos = lax.broadcasted_iota(I32, (C, 1), 0).astype(F32)
    for h in range(n_heads):
        lgf = lgf_ref[h]
        q = q_ref[:, h * RET_QK_DIM:(h + 1) * RET_QK_DIM].astype(BF16)
        k = k_ref[:, h * RET_QK_DIM:(h + 1) * RET_QK_DIM]
        v = v_ref[:, h * RET_V_DIM:(h + 1) * RET_V_DIM].astype(BF16)
        scores = _dot_nt(q, k.astype(BF16)) * dmat_ref[h]
        inner = _dot(scores.astype(BF16), v)
        st = state_ref[h]
        cross = _dot(q, st.astype(BF16)) * jnp.exp((pos + 1.0) * lgf)
        o_ref[:, h * RET_V_DIM:(h + 1) * RET_V_DIM] = inner + cross
        kw = k * jnp.exp((C - 1.0 - pos) * lgf)
        kv = _dot(kw.T.astype(BF16), v)
        state_ref[h] = st * jnp.exp(jnp.zeros((1, RET_V_DIM), F32) + C * lgf) + kv


def _ret_bwd_body(lgb_ref, q_ref, k_ref, v_ref, rg_ref, o1_ref, g_ref, o_ref, state_ref, *, n_heads, n_chunks, end_chunks):
    ci = n_chunks - 1 - pl.program_id(0)
    C = RET_CHUNK

    @pl.when(_any_of(ci, end_chunks))
    def _():
        state_ref[...] = jnp.zeros_like(state_ref)

    pos = lax.broadcasted_iota(I32, (C, 1), 0).astype(F32)
    for h in range(n_heads):
        lgb = lgb_ref[h]
        vs = slice(h * RET_V_DIM, (h + 1) * RET_V_DIM)
        q = q_ref[:, h * RET_QK_DIM:(h + 1) * RET_QK_DIM].astype(BF16)
        k = k_ref[:, h * RET_QK_DIM:(h + 1) * RET_QK_DIM]
        v = v_ref[:, vs].astype(BF16)
        st = state_ref[h]
        cross = _dot(q, st.astype(BF16)) * jnp.exp((C - pos) * lgb)
        o = o1_ref[:, vs] + cross
        ms = jnp.mean(o * o, axis=-1, keepdims=True)
        r = o * lax.rsqrt(ms + RMS_EPS) * g_ref[:, vs]
        gate = rg_ref[:, vs]
        r = r * (gate / (1.0 + jnp.exp(-gate)))
        o_ref[:, vs] = r.astype(o_ref.dtype)
        kw = k * jnp.exp(pos * lgb)
        kv = _dot(kw.T.astype(BF16), v)
        state_ref[h] = st * jnp.exp(jnp.zeros((1, RET_V_DIM), F32) + C * lgb) + kv


def _retention(proj, lg_f, lg_b, ret_norm_g, n_heads, seq_lens):
    n = proj.shape[0]
    C = RET_CHUNK
    qw = n_heads * RET_QK_DIM
    vw = n_heads * RET_V_DIM
    n_chunks = n // C
    starts, ends, acc = [], [], 0
    for t in seq_lens:
        starts.append(acc // C)
        acc += t
        ends.append(acc // C - 1)
    smem = pl.BlockSpec(memory_space=pltpu.SMEM)
    o1 = pl.pallas_call(
        functools.partial(_ret_fwd_body, n_heads=n_heads, start_chunks=tuple(starts)),
        grid=(n_chunks,),
        in_specs=[
            smem, smem,
            pl.BlockSpec((C, qw), lambda c: (c, 0)),
            pl.BlockSpec((C, qw), lambda c: (c, 1)),
            pl.BlockSpec((C, vw), lambda c: (c, 1)),
        ],
        out_specs=pl.BlockSpec((C, vw), lambda c: (c, 0)),
        out_shape=jax.ShapeDtypeStruct((n, vw), F32),
        scratch_shapes=[pltpu.VMEM((n_heads, C, C), F32), pltpu.VMEM((n_heads, RET_QK_DIM, RET_V_DIM), F32)],
        compiler_params=_cparams(1, 32),
        name="retention_fwd",
    )(lg_f, lg_b, proj, proj, proj)
    last = n_chunks - 1
    return pl.pallas_call(
        functools.partial(_ret_bwd_body, n_heads=n_heads, n_chunks=n_chunks, end_chunks=tuple(ends)),
        grid=(n_chunks,),
        in_specs=[
            smem,
            pl.BlockSpec((C, qw), lambda c: (last - c, 0)),
            pl.BlockSpec((C, qw), lambda c: (last - c, 1)),
            pl.BlockSpec((C, vw), lambda c: (last - c, 1)),
            pl.BlockSpec((C, vw), lambda c: (last - c, 2)),
            pl.BlockSpec((C, vw), lambda c: (last - c, 0)),
            pl.BlockSpec((1, vw), lambda c: (0, 0)),
        ],
        out_specs=pl.BlockSpec((C, vw), lambda c: (last - c, 0)),
        out_shape=jax.ShapeDtypeStruct((n, vw), BF16),
        scratch_shapes=[pltpu.VMEM((n_heads, RET_QK_DIM, RET_V_DIM), F32)],
        compiler_params=_cparams(1, 32),
        name="retention_bwd",
    )(lg_b, proj, proj, proj, proj, o1, ret_norm_g.reshape(1, vw))


def _sub_rows(ref, start, count, residue, dil):
    if dil == 1:
        return ref[start:start + count, :]
    return ref[pl.ds(residue + dil * start, count, stride=dil), :]


def _attn_unit(q, k, v, bias_scale, lo, hi):
    R, W = q.shape[0], k.shape[0]
    s = _dot_nt(q.astype(BF16), k.astype(BF16))
    ii = lax.broadcasted_iota(I32, (R, W), 0)
    jj = lax.broadcasted_iota(I32, (R, W), 1)
    dist = jnp.abs(jj - DIL_RADIUS - ii)
    valid = (dist <= DIL_RADIUS) & (jj >= lo) & (jj < hi)
    s = jnp.where(valid, s - bias_scale * dist.astype(F32), MASKED)
    m = jnp.max(s, axis=-1, keepdims=True)
    p = jnp.exp(s - m)
    l = jnp.sum(p, axis=-1, keepdims=True)
    o = _dot(p.astype(BF16), v.astype(BF16)) / l
    return o, m + jnp.log(l)


def _attn_body(slopes_ref, q_ref, kp_ref, kc_ref, kn_ref, vp_ref, vc_ref, vn_ref, o_ref, acc_ref, lse_ref,
               *, tq, start_tiles, end_tiles):
    i = pl.program_id(0)
    slope = slopes_ref[pl.program_id(1)]
    no_prev = jnp.where(_any_of(i, start_tiles), 1, 0)
    no_next = jnp.where(_any_of(i, end_tiles), 1, 0)
    rad = DIL_RADIUS
    for bi, (_, dil) in enumerate(DIL_BRANCHES):
        sub = tq // dil
        R = min(256, sub)
        for residue in range(dil):
            for u in range(sub // R):
                s0 = u * R
                q = _sub_rows(q_ref, s0, R, residue, dil)
                first, final = u == 0, u == sub // R - 1
                k_parts, v_parts = [], []
                if first:
                    k_parts.append(_sub_rows(kp_ref, sub - rad, rad, residue, dil))
                    v_parts.append(_sub_rows(vp_ref, sub - rad, rad, residue, dil))
                a = s0 - (0 if first else rad)
                b = s0 + R + (0 if final else rad)
                k_parts.append(_sub_rows(kc_ref, a, b - a, residue, dil))
                v_parts.append(_sub_rows(vc_ref, a, b - a, residue, dil))
                if final:
                    k_parts.append(_sub_rows(kn_ref, 0, rad, residue, dil))
                    v_parts.append(_sub_rows(vn_ref, 0, rad, residue, dil))
                k = jnp.concatenate(k_parts, axis=0) if len(k_parts) > 1 else k_parts[0]
                v = jnp.concatenate(v_parts, axis=0) if len(v_parts) > 1 else v_parts[0]
                lo = rad * no_prev if first else 0
                hi = R + 2 * rad - (rad * no_next if final else 0)
                o, lse = _attn_unit(q, k, v, slope * float(dil), lo, hi)
                lse_b = jnp.broadcast_to(lse, (R, LANES))
                if dil == 1:
                    acc_ref[bi, s0:s0 + R, :] = o
                    lse_ref[bi, s0:s0 + R, :] = lse_b
                else:
                    rows = pl.ds(residue + dil * s0, R, stride=dil)
                    acc_ref[bi, rows, :] = o
                    lse_ref[bi, rows, :] = lse_b
    n_br = len(DIL_BRANCHES)
    m = lse_ref[0]
    for bi in range(1, n_br):
        m = jnp.maximum(m, lse_ref[bi])
    num = jnp.zeros((tq, LANES), F32)
    den = jnp.zeros((tq, LANES), F32)
    for bi in range(n_br):
        w = jnp.exp(lse_ref[bi] - m)
        num = num + w * acc_ref[bi]
        den = den + w
    o_ref[...] = num / den


def _dilated_attention(proj, slopes, n_heads, q_blk, seq_lens, tq):
    n = proj.shape[0]
    nt = n // tq
    starts, ends, acc = [], [], 0
    for t in seq_lens:
        assert t % tq == 0
        starts.append(acc // tq)
        acc += t
        ends.append(acc // tq - 1)
    qc, kc, vc = q_blk * n_heads, (q_blk + 1) * n_heads, (q_blk + 2) * n_heads
    blk = (tq, HEAD_DIM)
    prev = lambda i: jnp.maximum(i - 1, 0)
    nxt = lambda i: jnp.minimum(i + 1, nt - 1)
    body = functools.partial(_attn_body, tq=tq, start_tiles=tuple(starts), end_tiles=tuple(ends))
    return pl.pallas_call(
        body,
        grid=(nt, n_heads),
        in_specs=[
            pl.BlockSpec(memory_space=pltpu.SMEM),
            pl.BlockSpec(blk, lambda i, h: (i, qc + h)),
            pl.BlockSpec(blk, lambda i, h: (prev(i), kc + h)),
            pl.BlockSpec(blk, lambda i, h: (i, kc + h)),
            pl.BlockSpec(blk, lambda i, h: (nxt(i), kc + h)),
            pl.BlockSpec(blk, lambda i, h: (prev(i), vc + h)),
            pl.BlockSpec(blk, lambda i, h: (i, vc + h)),
            pl.BlockSpec(blk, lambda i, h: (nxt(i), vc + h)),
        ],
        out_specs=pl.BlockSpec(blk, lambda i, h: (i, h)),
        out_shape=jax.ShapeDtypeStruct((n, n_heads * HEAD_DIM), F32),
        scratch_shapes=[pltpu.VMEM((len(DIL_BRANCHES), tq, LANES), F32)] * 2,
        compiler_params=_cparams(2, 32),
        name="dilated_attention",
    )(slopes, proj, proj, proj, proj, proj, proj, proj)


def _out_proj_body(ret_ref, att_ref, g_ref, w_ref, x_ref, o_ref, cat_ref, *, ret_w):
    @pl.when(pl.program_id(1) == 0)
    def _():
        cat_ref[:, :ret_w] = ret_ref[...]
        a = att_ref[...]
        ms = jnp.mean(a * a, axis=-1, keepdims=True)
        cat_ref[:, ret_w:] = (a * lax.rsqrt(ms + RMS_EPS) * g_ref[...]).astype(BF16)

    o_ref[...] = x_ref[...] + _dot(cat_ref[...], w_ref[...])


def _out_proj(ret, att, attn_out_g, w_out, x, tm, tn):
    n, d = x.shape
    ret_w, attn_w = ret.shape[1], att.shape[1]
    return pl.pallas_call(
        functools.partial(_out_proj_body, ret_w=ret_w),
        grid=(n // tm, d // tn),
        in_specs=[
            pl.BlockSpec((tm, ret_w), lambda i, j: (i, 0)),
            pl.BlockSpec((tm, attn_w), lambda i, j: (i, 0)),
            pl.BlockSpec((1, attn_w), lambda i, j: (0, 0)),
            pl.BlockSpec((ret_w + attn_w, tn), lambda i, j: (0, j)),
            pl.BlockSpec((tm, tn), lambda i, j: (i, j)),
        ],
        out_specs=pl.BlockSpec((tm, tn), lambda i, j: (i, j)),
        out_shape=jax.ShapeDtypeStruct((n, d), F32),
        scratch_shapes=[pltpu.VMEM((tm, ret_w + attn_w), BF16)],
        compiler_params=_cparams(2, 56),
        name="out_proj",
    )(ret, att, attn_out_g.reshape(1, attn_w), w_out, x)


def _ffn_norm_router_body(x_ref, g_ref, wr_ref, h_ref, lg_ref):
    x = x_ref[...]
    ms = jnp.mean(x * x, axis=-1, keepdims=True)
    h = x * lax.rsqrt(ms + RMS_EPS) * g_ref[...]
    h_ref[...] = h
    lg_ref[...] = jnp.dot(h, wr_ref[...], preferred_element_type=F32, precision=lax.Precision.HIGHEST)


def _ffn_norm_router(x1, g, w_router, tm):
    n, d = x1.shape
    e = w_router.shape[1]
    return pl.pallas_call(
        _ffn_norm_router_body,
        grid=(n // tm,),
        in_specs=[
            pl.BlockSpec((tm, d), lambda i: (i, 0)),
            pl.BlockSpec((1, d), lambda i: (0, 0)),
            pl.BlockSpec((d, e), lambda i: (0, 0)),
        ],
        out_specs=[pl.BlockSpec((tm, d), lambda i: (i, 0)), pl.BlockSpec((tm, e), lambda i: (i, 0))],
        out_shape=[jax.ShapeDtypeStruct((n, d), F32), jax.ShapeDtypeStruct((n, e), F32)],
        compiler_params=_cparams(1, 48),
        name="ffn_norm_router",
    )(x1, g.reshape(1, d), w_router)


def _exclusive_cumsum_tokens(x, upper, lower):
    e, nb, _ = x.shape
    within = _dot(x.reshape(e * nb, LANES).astype(BF16), upper).reshape(e, nb, LANES)
    tot = jnp.sum(x, axis=2, keepdims=True)
    tot_b = jnp.concatenate([jnp.broadcast_to(tot[ei], (nb, LANES)) for ei in range(e)], axis=1)
    off_b = _dot(lower, tot_b.astype(BF16))
    off = jnp.stack([off_b[:, ei * LANES:(ei + 1) * LANES] for ei in range(e)], axis=0)
    return within + off, off


def _route_select_body(lt_ref, aff_ref, sel_ref, pos_ref, off_ref, *, cap):
    lt = lt_ref[...]
    e, nb, _ = lt.shape
    m = jnp.max(lt, axis=0, keepdims=True)
    ex = jnp.exp(lt - m)
    aff = ex / jnp.sum(ex, axis=0, keepdims=True)
    aff_ref[...] = aff
    bits = pltpu.bitcast(aff, I32)

    def count(mask):
        c = jnp.sum(jnp.where(mask, 1.0, 0.0), axis=2, keepdims=True)
        return jnp.sum(c, axis=1, keepdims=True)

    thr = jnp.zeros((e, 1, 1), I32)
    for b in range(30, -1, -1):
        cand = thr | (1 << b)
        thr = jnp.where(count(bits >= cand) >= cap, cand, thr)

    ii = lax.broadcasted_iota(I32, (LANES, LANES), 0)
    jj = lax.broadcasted_iota(I32, (LANES, LANES), 1)
    upper = jnp.where(ii < jj, 1.0, 0.0).astype(BF16)
    bi = lax.broadcasted_iota(I32, (nb, nb), 0)
    bj = lax.broadcasted_iota(I32, (nb, nb), 1)
    lower = jnp.where(bj < bi, 1.0, 0.0).astype(BF16)

    gt = bits > thr
    eq = bits == thr
    need = cap - count(gt)
    eq_rank, _ = _exclusive_cumsum_tokens(jnp.where(eq, 1.0, 0.0), upper, lower)
    sel = jnp.where(gt | (eq & (eq_rank < need)), 1.0, 0.0)
    pos, off = _exclusive_cumsum_tokens(sel, upper, lower)
    sel_ref[...] = sel
    pos_ref[...] = pos
    off_ref[...] = off.astype(I32)


def _route_compact_body(off_ref, sel_ref, pos_ref, aff_ref, idx_ref, gate_ref, acc_i, acc_g, *, nb, n_tiles):
    e = pl.program_id(0)
    acc_i[...] = jnp.zeros_like(acc_i)
    acc_g[...] = jnp.zeros_like(acc_g)
    lane = lax.broadcasted_iota(I32, (1, LANES), 1).astype(F32)
    slot = lax.broadcasted_iota(I32, (LANES, 1), 0).astype(F32)

    def block(b, carry):
        first = off_ref[e, b] // LANES
        pos = pos_ref[0, pl.ds(b, 1), :]
        chosen = sel_ref[0, pl.ds(b, 1), :] > 0.0
        aff = aff_ref[0, pl.ds(b, 1), :]
        tok = lane + lax.convert_element_type(b * LANES, F32)
        for dj in range(2):
            t = first + dj
            match = chosen & (pos == slot + lax.convert_element_type(t * LANES, F32))
            acc_i[t] = acc_i[t] + jnp.where(match, tok, 0.0)
            acc_g[t] = acc_g[t] + jnp.where(match, aff, 0.0)
        return carry

    lax.fori_loop(0, nb, block, 0)
    for t in range(n_tiles):
        rows = slice(t * LANES, (t + 1) * LANES)
        idx_ref[0, rows, :] = jnp.broadcast_to(jnp.sum(acc_i[t], axis=1, keepdims=True), (LANES, LANES))
        gate_ref[0, rows, :] = jnp.broadcast_to(jnp.sum(acc_g[t], axis=1, keepdims=True), (LANES, LANES))


def _route(logits, cap):
    n, e = logits.shape
    nb = n // LANES
    n_tiles = cap // LANES
    lt = logits.T.reshape(e, nb, LANES)
    shp = jax.ShapeDtypeStruct((e, nb, LANES), F32)
    aff, sel, pos, off = pl.pallas_call(
        functools.partial(_route_select_body, cap=cap),
        out_shape=[shp, shp, shp, jax.ShapeDtypeStruct((e, nb, LANES), I32)],
        compiler_params=pltpu.CompilerParams(vmem_limit_bytes=48 * 1024 * 1024),
        name="route_select",
    )(lt)
    blk = pl.BlockSpec((1, nb, LANES), lambda ei, off_ref: (ei, 0, 0))
    oblk = pl.BlockSpec((1, cap, LANES), lambda ei, off_ref: (ei, 0, 0))
    oshp = jax.ShapeDtypeStruct((e, cap, LANES), F32)
    idx_b, gate_b = pl.pallas_call(
        functools.partial(_route_compact_body, nb=nb, n_tiles=n_tiles),
        grid_spec=pltpu.PrefetchScalarGridSpec(
            num_scalar_prefetch=1,
            grid=(e,),
            in_specs=[blk, blk, blk],
            out_specs=[oblk, oblk],
            scratch_shapes=[pltpu.VMEM((n_tiles + 2, LANES, LANES), F32)] * 2,
        ),
        out_shape=[oshp, oshp],
        compiler_params=_cparams(1, 32),
        name="route_compact",
    )(off[:, :, 0], sel, pos, aff)
    return idx_b[:, :, 0].astype(I32), gate_b


def _gather_body(idx_ref, h_hbm, o_ref, buf, sem, *, rows):
    base = pl.program_id(0) * rows

    def issue(r, carry):
        pltpu.make_async_copy(h_hbm.at[pl.ds(idx_ref[base + r], 1), :], buf.at[pl.ds(r, 1), :], sem).start()
        return carry

    lax.fori_loop(0, rows, issue, 0)
    pltpu.make_async_copy(h_hbm.at[pl.ds(0, rows), :], buf, sem).wait()
    o_ref[...] = buf[...].astype(o_ref.dtype)


def _gather_rows(h, idx_flat, rows):
    n, d = h.shape
    total = idx_flat.shape[0]
    return pl.pallas_call(
        functools.partial(_gather_body, rows=rows),
        grid_spec=pltpu.PrefetchScalarGridSpec(
            num_scalar_prefetch=1,
            grid=(total // rows,),
            in_specs=[pl.BlockSpec(memory_space=pl.ANY)],
            out_specs=pl.BlockSpec((rows, d), lambda s, idx: (s, 0)),
            scratch_shapes=[pltpu.VMEM((rows, d), F32), pltpu.SemaphoreType.DMA(())],
        ),
        out_shape=jax.ShapeDtypeStruct((total, d), BF16),
        compiler_params=_cparams(1, 32),
        name="gather_rows",
    )(idx_flat, h)


def _moe_body(x_ref, wg_ref, wu_ref, wd_ref, gate_ref, o_ref, *, nf):
    f = pl.program_id(2)

    @pl.when(f == 0)
    def _():
        o_ref[...] = jnp.zeros_like(o_ref)

    x = x_ref[...]
    g = _dot(x, wg_ref[...].astype(BF16))
    u = _dot(x, wu_ref[...].astype(BF16))
    hid = (g / (1.0 + jnp.exp(-g)) * u).astype(BF16)
    for c in range(o_ref.shape[1] // MOE_OUT_CHUNK):
        cols = slice(c * MOE_OUT_CHUNK, (c + 1) * MOE_OUT_CHUNK)
        o_ref[:, cols] += _dot(hid, wd_ref[:, cols].astype(BF16))

    @pl.when(f == nf - 1)
    def _():
        o_ref[...] = o_ref[...] * gate_ref[:, 0:1]


def _moe(xe, w_gate, w_up, w_down, gate_b, tm, tf):
    e, d, fdim = w_gate.shape
    total = xe.shape[0]
    tiles = total // e // tm
    nf = fdim // tf
    assert nf >= 2 and fdim % tf == 0
    row = lambda ei, s, f: (ei * tiles + s, 0)
    return pl.pallas_call(
        functools.partial(_moe_body, nf=nf),
        grid=(e, tiles, nf),
        in_specs=[
            pl.BlockSpec((tm, d), row, pipeline_mode=pl.Buffered(1)),
            pl.BlockSpec((None, d, tf), lambda ei, s, f: (ei, 0, f)),
            pl.BlockSpec((None, d, tf), lambda ei, s, f: (ei, 0, f)),
            pl.BlockSpec((None, tf, d), lambda ei, s, f: (ei, f, 0)),
            pl.BlockSpec((tm, LANES), row),
        ],
        out_specs=pl.BlockSpec((tm, d), row, pipeline_mode=pl.Buffered(1)),
        out_shape=jax.ShapeDtypeStruct((total, d), F32),
        compiler_params=_cparams(3, 60),
        name="expert_swiglu",
    )(xe, w_gate, w_up, w_down, gate_b)


def _scatter_body(idx_ref, ye_ref, y_in, y_hbm, buf, sem_in, sem_out, *, rows):
    del y_in
    base = pl.program_id(0) * rows

    def fetch(r, carry):
        pltpu.make_async_copy(y_hbm.at[pl.ds(idx_ref[base + r], 1), :], buf.at[pl.ds(r, 1), :], sem_in).start()
        return carry

    lax.fori_loop(0, rows, fetch, 0)
    pltpu.make_async_copy(y_hbm.at[pl.ds(0, rows), :], buf, sem_in).wait()
    buf[...] = buf[...] + ye_ref[...]

    def put(r, carry):
        pltpu.make_async_copy(buf.at[pl.ds(r, 1), :], y_hbm.at[pl.ds(idx_ref[base + r], 1), :], sem_out).start()
        return carry

    lax.fori_loop(0, rows, put, 0)
    pltpu.make_async_copy(buf, y_hbm.at[pl.ds(0, rows), :], sem_out).wait()


def _scatter_add(y, ye, idx_flat, rows):
    n, d = y.shape
    total = ye.shape[0]
    return pl.pallas_call(
        functools.partial(_scatter_body, rows=rows),
        grid_spec=pltpu.PrefetchScalarGridSpec(
            num_scalar_prefetch=1,
            grid=(total // rows,),
            in_specs=[pl.BlockSpec((rows, d), lambda s, idx: (s, 0)), pl.BlockSpec(memory_space=pl.ANY)],
            out_specs=pl.BlockSpec(memory_space=pl.ANY),
            scratch_shapes=[pltpu.VMEM((rows, d), F32), pltpu.SemaphoreType.DMA(()), pltpu.SemaphoreType.DMA(())],
        ),
        out_shape=jax.ShapeDtypeStruct((n, d), F32),
        input_output_aliases={2: 0},
        compiler_params=_cparams(1, 32),
        name="scatter_add",
    )(idx_flat, ye, y)


def _tile(n, pref):
    t = min(n, pref)
    assert n % t == 0
    return t


def _encoder_layer(x, group_tokens, seq_lens, attn_norm_g, w_in, ret_decay_fwd, ret_decay_bwd, ret_norm_g,
                   q_norm_g, k_norm_g, attn_out_g, w_out, ffn_norm_g, w_router, w_gate, w_up, w_down):
    n, d = x.shape
    n_ret = ret_decay_fwd.shape[0]
    ret_q_w, ret_v_w = n_ret * RET_QK_DIM, n_ret * RET_V_DIM
    attn_w = attn_out_g.shape[0]
    n_attn = attn_w // HEAD_DIM
    assert ret_v_w == attn_w and w_in.shape[1] == 2 * ret_q_w + 2 * ret_v_w + 3 * attn_w
    n_exp = w_router.shape[1]

    h = _rmsnorm_rows(x, attn_norm_g, BF16, _tile(n, 512))
    proj = _in_proj(h, w_in.astype(BF16), q_norm_g, k_norm_g, ret_q_w, ret_v_w, attn_w,
                    _tile(n, 1024), _tile(ret_q_w, 1024))
    lg_f = jnp.log1p(-jnp.exp2(-ret_decay_fwd.astype(F32)))
    lg_b = jnp.log1p(-jnp.exp2(-ret_decay_bwd.astype(F32)))
    ret = _retention(proj, lg_f, lg_b, ret_norm_g, n_ret, seq_lens)
    slopes = jnp.exp2(-8.0 * jnp.arange(1, n_attn + 1, dtype=F32) / n_attn)
    att = _dilated_attention(proj, slopes, n_attn, 3, seq_lens, 1024)
    x1 = _out_proj(ret, att, attn_out_g, w_out.astype(BF16), x, _tile(n, 512), _tile(d, 1024))

    h2, logits = _ffn_norm_router(x1, ffn_norm_g, w_router, _tile(n, 512))
    idx_parts, gate_parts, start = [], [], 0
    for tokens in group_tokens:
        cap = EXPERT_CAPACITY * tokens // n_exp
        idx_g, gate_g = _route(logits[start:start + tokens], cap)
        idx_parts.append(idx_g + start)
        gate_parts.append(gate_g)
        start += tokens
    idx = jnp.concatenate(idx_parts, axis=1)
    gate_b = jnp.concatenate(gate_parts, axis=1)
    slots = idx.shape[1]
    idx_flat = idx.reshape(n_exp * slots)
    tm = math.gcd(1024, *[EXPERT_CAPACITY * t // n_exp for t in group_tokens])
    xe = _gather_rows(h2, idx_flat, _tile(tm, 128))
    ye = _moe(xe, w_gate, w_up, w_down, gate_b.reshape(n_exp * slots, LANES), tm, 256)
    return _scatter_add(x1, ye, idx_flat, _tile(tm, 128))


def kernel(x_prompt, x_sample, attn_norm_g, w_in, ret_decay_fwd, ret_decay_bwd, ret_norm_g, q_norm_g, k_norm_g,
           attn_out_g, w_out, ffn_norm_g, w_router, w_gate, w_up, w_down):
    depth = w_in.shape[0]
    d = x_prompt.shape[-1]
    bp, tp, _ = x_prompt.shape
    bs, ts, _ = x_sample.shape
    n_p, n_s = bp * tp, bs * ts
    x = jnp.concatenate([x_prompt.reshape(n_p, d), x_sample.reshape(n_s, d)], axis=0)
    seq_lens = (tp,) * bp + (ts,) * bs
    for l in range(depth):
        x = _encoder_layer(
            x, (n_p, n_s), seq_lens, attn_norm_g[l], w_in[l], ret_decay_fwd[l], ret_decay_bwd[l], ret_norm_g[l],
            q_norm_g[l], k_norm_g[l], attn_out_g[l], w_out[l], ffn_norm_g[l], w_router[l],
            w_gate[l], w_up[l], w_down[l])
    return x[:n_p].reshape(bp, tp, d), x[n_p:].reshape(bs, ts, d)
```

```python
import functools
import math

import jax
import jax.numpy as jnp
from jax import lax
from jax.experimental import pallas as pl
from jax.experimental.pallas import tpu as pltpu

F32 = jnp.float32
BF16 = jnp.bfloat16
I32 = jnp.int32

RMS_EPS = 1e-6
RET_QK_DIM = 128
RET_V_DIM = 256
HEAD_DIM = 128
RET_CHUNK = 128
DIL_BRANCHES = ((128, 1), (512, 4), (2048, 16))
DIL_RADIUS = 64
EXPERT_CAPACITY = 2
LANES = 128
MASKED = -1e30
MOE_OUT_CHUNK = 512

V7X_VMEM_BYTES = 64 * 1024 * 1024


def _cparams(n_axes, vmem_mb):
    assert vmem_mb * 1024 * 1024 <= V7X_VMEM_BYTES
    return pltpu.CompilerParams(
        dimension_semantics=("arbitrary",) * n_axes,
        vmem_limit_bytes=vmem_mb * 1024 * 1024,
    )


def _any_of(idx, values):
    out = idx == values[0]
    for v in values[1:]:
        out = jnp.logical_or(out, idx == v)
    return out


def _dot(a, b):
    return jnp.dot(a, b, preferred_element_type=F32)


def _dot_nt(a, b):
    return lax.dot_general(a, b, (((1,), (1,)), ((), ())), preferred_element_type=F32)


def _rmsnorm_body(xa_ref, xb_ref, g_ref, o_ref, *, a_tiles):
    def norm(x_ref):
        x = x_ref[...]
        ms = jnp.mean(x * x, axis=-1, keepdims=True)
        o_ref[...] = (x * lax.rsqrt(ms + RMS_EPS) * g_ref[...]).astype(o_ref.dtype)

    @pl.when(pl.program_id(0) < a_tiles)
    def _():
        norm(xa_ref)

    @pl.when(pl.program_id(0) >= a_tiles)
    def _():
        norm(xb_ref)


def _rmsnorm_rows(xa, xb, g, out_dtype, tm):
    (na, d), nb = xa.shape, xb.shape[0]
    assert na % tm == 0 and nb % tm == 0
    a_tiles = na // tm
    return pl.pallas_call(
        functools.partial(_rmsnorm_body, a_tiles=a_tiles),
        grid=((na + nb) // tm,),
        in_specs=[
            pl.BlockSpec((tm, d), lambda i: (jnp.minimum(i, a_tiles - 1), 0)),
            pl.BlockSpec((tm, d), lambda i: (jnp.maximum(i - a_tiles, 0), 0)),
            pl.BlockSpec((1, d), lambda i: (0, 0)),
        ],
        out_specs=pl.BlockSpec((tm, d), lambda i: (i, 0)),
        out_shape=jax.ShapeDtypeStruct((na + nb, d), out_dtype),
        compiler_params=_cparams(1, 56),
        name="rmsnorm",
    )(xa, xb, g.reshape(1, d))


def _head_rmsnorm(x, g_row, scale):
    parts = []
    for c in range(x.shape[1] // HEAD_DIM):
        xs = x[:, c * HEAD_DIM:(c + 1) * HEAD_DIM]
        ms = jnp.mean(xs * xs, axis=-1, keepdims=True)
        parts.append(xs * lax.rsqrt(ms + RMS_EPS) * g_row * scale)
    return parts


def _in_proj_body(h_ref, w_ref, gq_ref, gk_ref, o_ref, *, rk_tiles, aq_tiles, ak_tiles):
    j = pl.program_id(1)
    acc = _dot(h_ref[...], w_ref[...])
    is_rk = jnp.logical_and(j >= rk_tiles[0], j < rk_tiles[1])
    is_aq = jnp.logical_and(j >= aq_tiles[0], j < aq_tiles[1])
    is_ak = jnp.logical_and(j >= ak_tiles[0], j < ak_tiles[1])
    is_plain = jnp.logical_not(jnp.logical_or(is_rk, jnp.logical_or(is_aq, is_ak)))

    @pl.when(is_plain)
    def _():
        o_ref[...] = acc

    @pl.when(is_rk)
    def _():
        o_ref[...] = acc * (RET_QK_DIM ** -0.5)

    @pl.when(is_aq)
    def _():
        for c, part in enumerate(_head_rmsnorm(acc, gq_ref[...], HEAD_DIM ** -0.5)):
            o_ref[:, c * HEAD_DIM:(c + 1) * HEAD_DIM] = part

    @pl.when(is_ak)
    def _():
        for c, part in enumerate(_head_rmsnorm(acc, gk_ref[...], 1.0)):
            o_ref[:, c * HEAD_DIM:(c + 1) * HEAD_DIM] = part


def _in_proj(h, w, gq, gk, ret_q_w, ret_v_w, attn_w, tm, tn):
    n, d = h.shape
    in_w = w.shape[1]
    offs = (ret_q_w, 2 * ret_q_w + 2 * ret_v_w, 2 * ret_q_w + 2 * ret_v_w + attn_w)
    for o in offs + (ret_q_w, attn_w):
        assert o % tn == 0
    rk_tiles = (offs[0] // tn, (offs[0] + ret_q_w) // tn)
    aq_tiles = (offs[1] // tn, (offs[1] + attn_w) // tn)
    ak_tiles = (offs[2] // tn, (offs[2] + attn_w) // tn)
    body = functools.partial(_in_proj_body, rk_tiles=rk_tiles, aq_tiles=aq_tiles, ak_tiles=ak_tiles)
    return pl.pallas_call(
        body,
        grid=(n // tm, in_w // tn),
        in_specs=[
            pl.BlockSpec((tm, d), lambda i, j: (i, 0)),
            pl.BlockSpec((d, tn), lambda i, j: (0, j)),
            pl.BlockSpec((1, HEAD_DIM), lambda i, j: (0, 0)),
            pl.BlockSpec((1, HEAD_DIM), lambda i, j: (0, 0)),
        ],
        out_specs=pl.BlockSpec((tm, tn), lambda i, j: (i, j)),
        out_shape=jax.ShapeDtypeStruct((n, in_w), F32),
        compiler_params=_cparams(2, 56),
        name="in_proj",
    )(h, w, gq.reshape(1, HEAD_DIM), gk.reshape(1, HEAD_DIM))


def _ret_fwd_body(lgf_ref, lgb_ref, q_ref, k_ref, v_ref, o_ref, dmat_ref, state_ref, *, n_heads, start_chunks):
    c = pl.program_id(0)
    C = RET_CHUNK

    @pl.when(c == 0)
    def _():
        row = lax.broadcasted_iota(I32, (C, C), 0)
        col = lax.broadcasted_iota(I32, (C, C), 1)
        diff = (row - col).astype(F32)
        for h in range(n_heads):
            fwd = jnp.where(diff >= 0, jnp.exp(jnp.maximum(diff, 0.0) * lgf_ref[h]), 0.0)
            bwd = jnp.where(diff < 0, jnp.exp(jnp.maximum(-diff, 0.0) * lgb_ref[h]), 0.0)
            dmat_ref[h] = fwd + bwd

    @pl.when(_any_of(c, start_chunks))
    def _():
        state_ref[...] = jnp.zeros_like(state_ref)

    pos = lax.broadcasted_iota(I32, (C, 1), 0).astype(F32)
    for h in range(n_heads):
        lgf = lgf_ref[h]
        q = q_ref[:, h * RET_QK_DIM:(h + 1) * RET_QK_DIM].astype(BF16)
        k = k_ref[:, h * RET_QK_DIM:(h + 1) * RET_QK_DIM]
        v = v_ref[:, h * RET_V_DIM:(h + 1) * RET_V_DIM].astype(BF16)
        scores = _dot_nt(q, k.astype(BF16)) * dmat_ref[h]
        inner = _dot(scores.astype(BF16), v)
        st = state_ref[h]
        cross = _dot(q, st.astype(BF16)) * jnp.exp((pos + 1.0) * lgf)
        o_ref[:, h * RET_V_DIM:(h + 1) * RET_V_DIM] = inner + cross
        kw = k * jnp.exp((C - 1.0 - pos) * lgf)
        kv = _dot(kw.T.astype(BF16), v)
        state_ref[h] = st * jnp.exp(jnp.zeros((1, RET_V_DIM), F32) + C * lgf) + kv


def _ret_bwd_body(lgb_ref, q_ref, k_ref, v_ref, rg_ref, o1_ref, g_ref, o_ref, state_ref, *, n_heads, n_chunks, end_chunks):
    ci = n_chunks - 1 - pl.program_id(0)
    C = RET_CHUNK

    @pl.when(_any_of(ci, end_chunks))
    def _():
        state_ref[...] = jnp.zeros_like(state_ref)

    pos = lax.broadcasted_iota(I32, (C, 1), 0).astype(F32)
    for h in range(n_heads):
        lgb = lgb_ref[h]
        vs = slice(h * RET_V_DIM, (h + 1) * RET_V_DIM)
        q = q_ref[:, h * RET_QK_DIM:(h + 1) * RET_QK_DIM].astype(BF16)
        k = k_ref[:, h * RET_QK_DIM:(h + 1) * RET_QK_DIM]
        v = v_ref[:, vs].astype(BF16)
        st = state_ref[h]
        cross = _dot(q, st.astype(BF16)) * jnp.exp((C - pos) * lgb)
        o = o1_ref[:, vs] + cross
        ms = jnp.mean(o * o, axis=-1, keepdims=True)
        r = o * lax.rsqrt(ms + RMS_EPS) * g_ref[:, vs]
        gate = rg_ref[:, vs]
        r = r * (gate / (1.0 + jnp.exp(-gate)))
        o_ref[:, vs] = r.astype(o_ref.dtype)
        kw = k * jnp.exp(pos * lgb)
        kv = _dot(kw.T.astype(BF16), v)
        state_ref[h] = st * jnp.exp(jnp.zeros((1, RET_V_DIM), F32) + C * lgb) + kv


def _retention(proj, lg_f, lg_b, ret_norm_g, n_heads, seq_lens):
    n = proj.shape[0]
    C = RET_CHUNK
    qw = n_heads * RET_QK_DIM
    vw = n_heads * RET_V_DIM
    n_chunks = n // C
    starts, ends, acc = [], [], 0
    for t in seq_lens:
        starts.append(acc // C)
        acc += t
        ends.append(acc // C - 1)
    smem = pl.BlockSpec(memory_space=pltpu.SMEM)
    o1 = pl.pallas_call(
        functools.partial(_ret_fwd_body, n_heads=n_heads, start_chunks=tuple(starts)),
        grid=(n_chunks,),
        in_specs=[
            smem, smem,
            pl.BlockSpec((C, qw), lambda c: (c, 0)),
            pl.BlockSpec((C, qw), lambda c: (c, 1)),
            pl.BlockSpec((C, vw), lambda c: (c, 1)),
        ],
        out_specs=pl.BlockSpec((C, vw), lambda c: (c, 0)),
        out_shape=jax.ShapeDtypeStruct((n, vw), F32),
        scratch_shapes=[pltpu.VMEM((n_heads, C, C), F32), pltpu.VMEM((n_heads, RET_QK_DIM, RET_V_DIM), F32)],
        compiler_params=_cparams(1, 32),
        name="retention_fwd",
    )(lg_f, lg_b, proj, proj, proj)
    last = n_chunks - 1
    return pl.pallas_call(
        functools.partial(_ret_bwd_body, n_heads=n_heads, n_chunks=n_chunks, end_chunks=tuple(ends)),
        grid=(n_chunks,),
        in_specs=[
            smem,
            pl.BlockSpec((C, qw), lambda c: (last - c, 0)),
            pl.BlockSpec((C, qw), lambda c: (last - c, 1)),
            pl.BlockSpec((C, vw), lambda c: (last - c, 1)),
            pl.BlockSpec((C, vw), lambda c: (last - c, 2)),
            pl.BlockSpec((C, vw), lambda c: (last - c, 0)),
            pl.BlockSpec((1, vw), lambda c: (0, 0)),
        ],
        out_specs=pl.BlockSpec((C, vw), lambda c: (last - c, 0)),
        out_shape=jax.ShapeDtypeStruct((n, vw), BF16),
        scratch_shapes=[pltpu.VMEM((n_heads, RET_QK_DIM, RET_V_DIM), F32)],
        compiler_params=_cparams(1, 32),
        name="retention_bwd",
    )(lg_b, proj, proj, proj, proj, o1, ret_norm_g.reshape(1, vw))


def _deinterleave(x, dil):
    return x if dil == 1 else pltpu.einshape("(ir)c->(ri)c", x, r=dil)


def _interleave(x, dil):
    return x if dil == 1 else pltpu.einshape("(ri)c->(ir)c", x, r=dil)


def _attn_unit_rows(tq, dil):
    return min(256, tq // dil)


def _attn_body(slopes_ref, q_ref, k0_ref, kn_ref, v0_ref, vn_ref, o_ref, kbuf, vbuf, qbuf, *bias_refs,
               tq, start_tiles, end_tiles):
    h, i = pl.program_id(0), pl.program_id(1)
    rad = DIL_RADIUS
    unit_rows = sorted({_attn_unit_rows(tq, dil) for _, dil in DIL_BRANCHES})
    bias_of = dict(zip(unit_rows, bias_refs))

    @pl.when(jnp.logical_and(h == 0, i == 0))
    def _():
        for R, ref in bias_of.items():
            ii = lax.broadcasted_iota(I32, (R, R + 2 * rad), 0)
            jj = lax.broadcasted_iota(I32, (R, R + 2 * rad), 1)
            dist = jnp.abs(jj - rad - ii)
            ref[...] = jnp.where(dist <= rad, -dist.astype(F32), MASKED)

    def fill(slot, k_ref, v_ref):
        kx, vx = k_ref[...], v_ref[...]
        for bi, (_, dil) in enumerate(DIL_BRANCHES):
            kbuf[slot, bi] = _deinterleave(kx, dil).astype(BF16)
            vbuf[slot, bi] = _deinterleave(vx, dil).astype(BF16)

    @pl.when(i == 0)
    def _():
        fill(0, k0_ref, v0_ref)
        kbuf[2] = jnp.zeros_like(kbuf[2])
        vbuf[2] = jnp.zeros_like(vbuf[2])

    cur, nxt, prv = i % 3, (i + 1) % 3, (i + 2) % 3
    fill(nxt, kn_ref, vn_ref)
    qx = q_ref[...]
    for bi, (_, dil) in enumerate(DIL_BRANCHES):
        qbuf[bi] = _deinterleave(qx, dil).astype(BF16)

    slope = slopes_ref[h]
    no_prev = jnp.where(_any_of(i, start_tiles), 1.0, 0.0)
    no_next = jnp.where(_any_of(i, end_tiles), 1.0, 0.0)

    groups = {}
    for bi, (_, dil) in enumerate(DIL_BRANCHES):
        sub = tq // dil
        R = _attn_unit_rows(tq, dil)
        for residue in range(dil):
            for u in range(sub // R):
                groups.setdefault(R, []).append((bi, dil, residue * sub, sub, u * R, u == 0, u == sub // R - 1))

    def window(buf, bi, base, sub, off, first, final, R):
        parts = []
        if first:
            parts.append(buf[prv, bi, base + sub - rad:base + sub, :])
        parts.append(buf[cur, bi, base + off - (0 if first else rad):base + off + R + (0 if final else rad), :])
        if final:
            parts.append(buf[nxt, bi, base:base + rad, :])
        return jnp.concatenate(parts, axis=0) if len(parts) > 1 else parts[0]

    outs, lses = [None] * len(DIL_BRANCHES), [None] * len(DIL_BRANCHES)
    for R, units in groups.items():
        B, W = len(units), R + 2 * rad
        q = jnp.stack([qbuf[bi, base + off:base + off + R, :] for bi, _, base, _, off, _, _ in units])
        k = jnp.stack([window(kbuf, bi, base, sub, off, fi, fa, R) for bi, _, base, sub, off, fi, fa in units])
        v = jnp.stack([window(vbuf, bi, base, sub, off, fi, fa, R) for bi, _, base, sub, off, fi, fa in units])
        b_idx = lax.broadcasted_iota(I32, (B, 1, 1), 0)
        col = lax.broadcasted_iota(I32, (1, 1, W), 2)
        dil_b = jnp.zeros((B, 1, 1), F32)
        for dil in sorted({u[1] for u in units}):
            dil_b = dil_b + jnp.where(_any_of(b_idx, [b for b, u in enumerate(units) if u[1] == dil]), float(dil), 0.0)
        first_b = jnp.where(_any_of(b_idx, [b for b, u in enumerate(units) if u[5]]), no_prev, 0.0)
        final_b = jnp.where(_any_of(b_idx, [b for b, u in enumerate(units) if u[6]]), no_next, 0.0)
        s = lax.dot_general(q, k, (((2,), (2,)), ((0,), (0,))), preferred_element_type=F32)
        s = s + (slope * dil_b) * bias_of[R][...][None]
        s = s + first_b * jnp.where(col < rad, MASKED, 0.0)
        s = s + final_b * jnp.where(col >= R + rad, MASKED, 0.0)
        m = jnp.max(s, axis=-1, keepdims=True)
        p = jnp.exp(s - m)
        l = jnp.sum(p, axis=-1, keepdims=True)
        o = lax.dot_general(p.astype(BF16), v, (((2,), (1,)), ((0,), (0,))), preferred_element_type=F32) / l
        lse = jnp.broadcast_to(m + jnp.log(l), (B, R, LANES))
        for bi, (_, dil) in enumerate(DIL_BRANCHES):
            mine = [b for b, u in enumerate(units) if u[0] == bi]
            if mine:
                lo, hi = mine[0], mine[-1] + 1
                outs[bi] = _interleave(o[lo:hi].reshape((hi - lo) * R, HEAD_DIM), dil)
                lses[bi] = _interleave(lse[lo:hi].reshape((hi - lo) * R, LANES), dil)
    m = functools.reduce(jnp.maximum, lses)
    num = jnp.zeros((tq, LANES), F32)
    den = jnp.zeros((tq, LANES), F32)
    for o, lse in zip(outs, lses):
        w = jnp.exp(lse - m)
        num = num + w * o
        den = den + w
    o_ref[...] = num / den


def _dilated_attention(proj, slopes, n_heads, q_blk, seq_lens, tq):
    n = proj.shape[0]
    nt = n // tq
    starts, ends, acc = [], [], 0
    for t in seq_lens:
        assert t % tq == 0
        starts.append(acc // tq)
        acc += t
        ends.append(acc // tq - 1)
    qc, kc, vc = q_blk * n_heads, (q_blk + 1) * n_heads, (q_blk + 2) * n_heads
    blk = (tq, HEAD_DIM)
    nxt = lambda i: jnp.minimum(i + 1, nt - 1)
    n_br = len(DIL_BRANCHES)
    unit_rows = sorted({_attn_unit_rows(tq, dil) for _, dil in DIL_BRANCHES})
    body = functools.partial(_attn_body, tq=tq, start_tiles=tuple(starts), end_tiles=tuple(ends))
    return pl.pallas_call(
        body,
        grid=(n_heads, nt),
        in_specs=[
            pl.BlockSpec(memory_space=pltpu.SMEM),
            pl.BlockSpec(blk, lambda h, i: (i, qc + h)),
            pl.BlockSpec(blk, lambda h, i: (0, kc + h)),
            pl.BlockSpec(blk, lambda h, i: (nxt(i), kc + h)),
            pl.BlockSpec(blk, lambda h, i: (0, vc + h)),
            pl.BlockSpec(blk, lambda h, i: (nxt(i), vc + h)),
        ],
        out_specs=pl.BlockSpec(blk, lambda h, i: (i, h)),
        out_shape=jax.ShapeDtypeStruct((n, n_heads * HEAD_DIM), F32),
        scratch_shapes=[
            pltpu.VMEM((3, n_br, tq, HEAD_DIM), BF16),
            pltpu.VMEM((3, n_br, tq, HEAD_DIM), BF16),
            pltpu.VMEM((n_br, tq, HEAD_DIM), BF16),
        ] + [pltpu.VMEM((r, r + 2 * DIL_RADIUS), F32) for r in unit_rows],
        compiler_params=_cparams(2, 32),
        name="dilated_attention",
    )(slopes, proj, proj, proj, proj, proj)


def _out_proj_body(ret_ref, att_ref, g_ref, w_ref, xa_ref, xb_ref, o_ref, cat_ref, *, ret_w, a_tiles):
    i = pl.program_id(0)

    @pl.when(pl.program_id(1) == 0)
    def _():
        cat_ref[:, :ret_w] = ret_ref[...]
        a = att_ref[...]
        ms = jnp.mean(a * a, axis=-1, keepdims=True)
        cat_ref[:, ret_w:] = (a * lax.rsqrt(ms + RMS_EPS) * g_ref[...]).astype(BF16)

    mixed = _dot(cat_ref[...], w_ref[...])

    @pl.when(i < a_tiles)
    def _():
        o_ref[...] = xa_ref[...] + mixed

    @pl.when(i >= a_tiles)
    def _():
        o_ref[...] = xb_ref[...] + mixed


def _out_proj(ret, att, attn_out_g, w_out, xa, xb, tm, tn):
    (na, d), nb = xa.shape, xb.shape[0]
    assert na % tm == 0 and nb % tm == 0
    a_tiles = na // tm
    ret_w, attn_w = ret.shape[1], att.shape[1]
    return pl.pallas_call(
        functools.partial(_out_proj_body, ret_w=ret_w, a_tiles=a_tiles),
        grid=((na + nb) // tm, d // tn),
        in_specs=[
            pl.BlockSpec((tm, ret_w), lambda i, j: (i, 0)),
            pl.BlockSpec((tm, attn_w), lambda i, j: (i, 0)),
            pl.BlockSpec((1, attn_w), lambda i, j: (0, 0)),
            pl.BlockSpec((ret_w + attn_w, tn), lambda i, j: (0, j)),
            pl.BlockSpec((tm, tn), lambda i, j: (jnp.minimum(i, a_tiles - 1), jnp.where(i < a_tiles, j, 0))),
            pl.BlockSpec((tm, tn), lambda i, j: (jnp.maximum(i - a_tiles, 0), jnp.where(i >= a_tiles, j, 0))),
        ],
        out_specs=pl.BlockSpec((tm, tn), lambda i, j: (i, j)),
        out_shape=jax.ShapeDtypeStruct((na + nb, d), F32),
        scratch_shapes=[pltpu.VMEM((tm, ret_w + attn_w), BF16)],
        compiler_params=_cparams(2, 56),
        name="out_proj",
    )(ret, att, attn_out_g.reshape(1, attn_w), w_out, xa, xb)


def _ffn_norm_router_body(x_ref, g_ref, wr_ref, h_ref, lg_ref):
    x = x_ref[...]
    ms = jnp.mean(x * x, axis=-1, keepdims=True)
    h = x * lax.rsqrt(ms + RMS_EPS) * g_ref[...]
    h_ref[...] = h
    lg_ref[...] = jnp.dot(h, wr_ref[...], preferred_element_type=F32, precision=lax.Precision.HIGHEST)


def _ffn_norm_router(x1, g, w_router, tm):
    n, d = x1.shape
    e = w_router.shape[1]
    return pl.pallas_call(
        _ffn_norm_router_body,
        grid=(n // tm,),
        in_specs=[
            pl.BlockSpec((tm, d), lambda i: (i, 0)),
            pl.BlockSpec((1, d), lambda i: (0, 0)),
            pl.BlockSpec((d, e), lambda i: (0, 0)),
        ],
        out_specs=[pl.BlockSpec((tm, d), lambda i: (i, 0)), pl.BlockSpec((tm, e), lambda i: (i, 0))],
        out_shape=[jax.ShapeDtypeStruct((n, d), F32), jax.ShapeDtypeStruct((n, e), F32)],
        compiler_params=_cparams(1, 48),
        name="ffn_norm_router",
    )(x1, g.reshape(1, d), w_router)


def _exclusive_cumsum_tokens(x, upper, lower):
    e, nb, _ = x.shape
    within = _dot(x.reshape(e * nb, LANES).astype(BF16), upper).reshape(e, nb, LANES)
    tot = jnp.sum(x, axis=2, keepdims=True)
    tot_b = jnp.concatenate([jnp.broadcast_to(tot[ei], (nb, LANES)) for ei in range(e)], axis=1)
    off_b = _dot(lower, tot_b.astype(BF16))
    off = jnp.stack([off_b[:, ei * LANES:(ei + 1) * LANES] for ei in range(e)], axis=0)
    return within + off, off


def _route_select_body(lt_ref, aff_ref, sel_ref, pos_ref, off_ref, *, cap):
    lt = lt_ref[...]
    e, nb, _ = lt.shape
    m = jnp.max(lt, axis=0, keepdims=True)
    ex = jnp.exp(lt - m)
    aff = ex / jnp.sum(ex, axis=0, keepdims=True)
    aff_ref[...] = aff
    bits = pltpu.bitcast(aff, I32)

    def count(mask):
        c = jnp.sum(jnp.where(mask, 1.0, 0.0), axis=2, keepdims=True)
        return jnp.sum(c, axis=1, keepdims=True)

    thr = jnp.zeros((e, 1, 1), I32)
    for b in range(30, -1, -1):
        cand = thr | (1 << b)
        thr = jnp.where(count(bits >= cand) >= cap, cand, thr)

    ii = lax.broadcasted_iota(I32, (LANES, LANES), 0)
    jj = lax.broadcasted_iota(I32, (LANES, LANES), 1)
    upper = jnp.where(ii < jj, 1.0, 0.0).astype(BF16)
    bi = lax.broadcasted_iota(I32, (nb, nb), 0)
    bj = lax.broadcasted_iota(I32, (nb, nb), 1)
    lower = jnp.where(bj < bi, 1.0, 0.0).astype(BF16)

    gt = bits > thr
    eq = bits == thr
    need = cap - count(gt)
    eq_rank, _ = _exclusive_cumsum_tokens(jnp.where(eq, 1.0, 0.0), upper, lower)
    sel = jnp.where(gt | (eq & (eq_rank < need)), 1.0, 0.0)
    pos, off = _exclusive_cumsum_tokens(sel, upper, lower)
    sel_ref[...] = sel
    pos_ref[...] = pos
    off_ref[...] = off.astype(I32)


def _route_compact_body(off_ref, sel_ref, pos_ref, aff_ref, idx_ref, gate_ref, acc_i, acc_g, *, nb, n_tiles):
    e = pl.program_id(0)
    acc_i[...] = jnp.zeros_like(acc_i)
    acc_g[...] = jnp.zeros_like(acc_g)
    lane = lax.broadcasted_iota(I32, (1, LANES), 1).astype(F32)
    slot = lax.broadcasted_iota(I32, (LANES, 1), 0).astype(F32)

    def block(b, carry):
        first = off_ref[e, b] // LANES
        pos = pos_ref[0, pl.ds(b, 1), :]
        chosen = sel_ref[0, pl.ds(b, 1), :] > 0.0
        aff = aff_ref[0, pl.ds(b, 1), :]
        tok = lane + lax.convert_element_type(b * LANES, F32)
        for dj in range(2):
            t = first + dj
            match = chosen & (pos == slot + lax.convert_element_type(t * LANES, F32))
            acc_i[t] = acc_i[t] + jnp.where(match, tok, 0.0)
            acc_g[t] = acc_g[t] + jnp.where(match, aff, 0.0)
        return carry

    lax.fori_loop(0, nb, block, 0)
    for t in range(n_tiles):
        rows = slice(t * LANES, (t + 1) * LANES)
        idx_ref[0, rows, :] = jnp.broadcast_to(jnp.sum(acc_i[t], axis=1, keepdims=True), (LANES, LANES))
        gate_ref[0, rows, :] = jnp.broadcast_to(jnp.sum(acc_g[t], axis=1, keepdims=True), (LANES, LANES))


def _route(logits, cap):
    n, e = logits.shape
    nb = n // LANES
    n_tiles = cap // LANES
    lt = logits.T.reshape(e, nb, LANES)
    shp = jax.ShapeDtypeStruct((e, nb, LANES), F32)
    aff, sel, pos, off = pl.pallas_call(
        functools.partial(_route_select_body, cap=cap),
        out_shape=[shp, shp, shp, jax.ShapeDtypeStruct((e, nb, LANES), I32)],
        compiler_params=pltpu.CompilerParams(vmem_limit_bytes=48 * 1024 * 1024),
        name="route_select",
    )(lt)
    blk = pl.BlockSpec((1, nb, LANES), lambda ei, off_ref: (ei, 0, 0))
    oblk = pl.BlockSpec((1, cap, LANES), lambda ei, off_ref: (ei, 0, 0))
    oshp = jax.ShapeDtypeStruct((e, cap, LANES), F32)
    idx_b, gate_b = pl.pallas_call(
        functools.partial(_route_compact_body, nb=nb, n_tiles=n_tiles),
        grid_spec=pltpu.PrefetchScalarGridSpec(
            num_scalar_prefetch=1,
            grid=(e,),
            in_specs=[blk, blk, blk],
            out_specs=[oblk, oblk],
            scratch_shapes=[pltpu.VMEM((n_tiles + 2, LANES, LANES), F32)] * 2,
        ),
        out_shape=[oshp, oshp],
        compiler_params=_cparams(1, 32),
        name="route_compact",
    )(off[:, :, 0], sel, pos, aff)
    return idx_b[:, :, 0].astype(I32), gate_b


def _gather_body(idx_ref, h_hbm, o_ref, buf, sem, *, rows, n_steps):
    s = pl.program_id(0)
    slot = s % 2

    def issue(step, into):
        def one(r, carry):
            src = h_hbm.at[pl.ds(idx_ref[step * rows + r], 1), :]
            pltpu.make_async_copy(src, buf.at[into, pl.ds(r, 1), :], sem.at[into]).start()
            return carry

        lax.fori_loop(0, rows, one, 0, unroll=8)

    @pl.when(s == 0)
    def _():
        issue(0, 0)

    @pl.when(s + 1 < n_steps)
    def _():
        issue(s + 1, 1 - slot)

    pltpu.make_async_copy(h_hbm.at[pl.ds(0, rows), :], buf.at[slot], sem.at[slot]).wait()
    o_ref[...] = buf[slot].astype(o_ref.dtype)


def _gather_rows(h, idx_flat, rows):
    n, d = h.shape
    total = idx_flat.shape[0]
    n_steps = total // rows
    return pl.pallas_call(
        functools.partial(_gather_body, rows=rows, n_steps=n_steps),
        grid_spec=pltpu.PrefetchScalarGridSpec(
            num_scalar_prefetch=1,
            grid=(n_steps,),
            in_specs=[pl.BlockSpec(memory_space=pl.ANY)],
            out_specs=pl.BlockSpec((rows, d), lambda s, idx: (s, 0)),
            scratch_shapes=[pltpu.VMEM((2, rows, d), F32), pltpu.SemaphoreType.DMA((2,))],
        ),
        out_shape=jax.ShapeDtypeStruct((total, d), BF16),
        compiler_params=_cparams(1, 32),
        name="gather_rows",
    )(idx_flat, h)


def _moe_body(x_ref, wg_ref, wu_ref, wd_ref, gate_ref, o_ref, *, nf):
    f = pl.program_id(2)

    @pl.when(f == 0)
    def _():
        o_ref[...] = jnp.zeros_like(o_ref)

    x = x_ref[...]
    g = _dot(x, wg_ref[...].astype(BF16))
    u = _dot(x, wu_ref[...].astype(BF16))
    hid = (g / (1.0 + jnp.exp(-g)) * u).astype(BF16)
    for c in range(o_ref.shape[1] // MOE_OUT_CHUNK):
        cols = slice(c * MOE_OUT_CHUNK, (c + 1) * MOE_OUT_CHUNK)
        o_ref[:, cols] += _dot(hid, wd_ref[:, cols].astype(BF16))

    @pl.when(f == nf - 1)
    def _():
        o_ref[...] = o_ref[...] * gate_ref[:, 0:1]


def _moe(xe, w_gate, w_up, w_down, gate_b, tm, tf):
    e, d, fdim = w_gate.shape
    total = xe.shape[0]
    tiles = total // e // tm
    nf = fdim // tf
    assert nf >= 2 and fdim % tf == 0
    row = lambda ei, s, f: (ei * tiles + s, 0)
    return pl.pallas_call(
        functools.partial(_moe_body, nf=nf),
        grid=(e, tiles, nf),
        in_specs=[
            pl.BlockSpec((tm, d), row, pipeline_mode=pl.Buffered(1)),
            pl.BlockSpec((None, d, tf), lambda ei, s, f: (ei, 0, f)),
            pl.BlockSpec((None, d, tf), lambda ei, s, f: (ei, 0, f)),
            pl.BlockSpec((None, tf, d), lambda ei, s, f: (ei, f, 0)),
            pl.BlockSpec((tm, LANES), row),
        ],
        out_specs=pl.BlockSpec((tm, d), row, pipeline_mode=pl.Buffered(1)),
        out_shape=jax.ShapeDtypeStruct((total, d), F32),
        compiler_params=_cparams(3, 60),
        name="expert_swiglu",
    )(xe, w_gate, w_up, w_down, gate_b)


def _scatter_body(idx_ref, ye_ref, y_in, y_hbm, buf, sem_in, sem_out, *, rows, n_steps, chunks):
    del y_in
    s = pl.program_id(0)
    c = s % chunks
    slot = s % 3

    def fetch(step):
        into = step % 3

        def one(r, carry):
            src = y_hbm.at[pl.ds(idx_ref[step * rows + r], 1), :]
            pltpu.make_async_copy(src, buf.at[into, pl.ds(r, 1), :], sem_in.at[into]).start()
            return carry

        lax.fori_loop(0, rows, one, 0, unroll=8)

    def write(step):
        out_of = step % 3

        def one(r, carry):
            dst = y_hbm.at[pl.ds(idx_ref[step * rows + r], 1), :]
            pltpu.make_async_copy(buf.at[out_of, pl.ds(r, 1), :], dst, sem_out.at[out_of]).start()
            return carry

        lax.fori_loop(0, rows, one, 0, unroll=8)

    def wait_write(ring_slot):
        pltpu.make_async_copy(buf.at[ring_slot], y_hbm.at[pl.ds(0, rows), :], sem_out.at[ring_slot]).wait()

    @pl.when(s == 0)
    def _():
        fetch(0)

    pltpu.make_async_copy(y_hbm.at[pl.ds(0, rows), :], buf.at[slot], sem_in.at[slot]).wait()
    buf[slot] = buf[slot] + ye_ref[...]
    write(s)

    @pl.when(c == 0)
    def _():
        fetch(s + 1)
        fetch(s + 2)

    @pl.when(jnp.logical_and(c >= 1, c <= chunks - 3))
    def _():
        wait_write((s + 2) % 3)
        fetch(s + 2)

    @pl.when(c == chunks - 1)
    def _():
        wait_write((s + 1) % 3)
        wait_write((s + 2) % 3)
        wait_write(slot)

        @pl.when(s + 1 < n_steps)
        def _():
            fetch(s + 1)


def _scatter_add(y, ye, idx_flat, rows, slots_per_expert):
    n, d = y.shape
    total = ye.shape[0]
    n_steps = total // rows
    chunks = slots_per_expert // rows
    assert chunks >= 3 and slots_per_expert % rows == 0
    return pl.pallas_call(
        functools.partial(_scatter_body, rows=rows, n_steps=n_steps, chunks=chunks),
        grid_spec=pltpu.PrefetchScalarGridSpec(
            num_scalar_prefetch=1,
            grid=(n_steps,),
            in_specs=[pl.BlockSpec((rows, d), lambda s, idx: (s, 0)), pl.BlockSpec(memory_space=pl.ANY)],
            out_specs=pl.BlockSpec(memory_space=pl.ANY),
            scratch_shapes=[pltpu.VMEM((3, rows, d), F32), pltpu.SemaphoreType.DMA((3,)),
                            pltpu.SemaphoreType.DMA((3,))],
        ),
        out_shape=jax.ShapeDtypeStruct((n, d), F32),
        input_output_aliases={2: 0},
        compiler_params=_cparams(1, 32),
        name="scatter_add",
    )(idx_flat, ye, y)


def _tile(n, pref):
    t = min(n, pref)
    assert n % t == 0
    return t


def _encoder_layer(xa, xb, seq_lens, attn_norm_g, w_in, ret_decay_fwd, ret_decay_bwd, ret_norm_g,
                   q_norm_g, k_norm_g, attn_out_g, w_out, ffn_norm_g, w_router, w_gate, w_up, w_down):
    group_tokens = (xa.shape[0], xb.shape[0])
    n, d = sum(group_tokens), xa.shape[1]
    n_ret = ret_decay_fwd.shape[0]
    ret_q_w, ret_v_w = n_ret * RET_QK_DIM, n_ret * RET_V_DIM
    attn_w = attn_out_g.shape[0]
    n_attn = attn_w // HEAD_DIM
    assert ret_v_w == attn_w and w_in.shape[1] == 2 * ret_q_w + 2 * ret_v_w + 3 * attn_w
    n_exp = w_router.shape[1]

    row_tile = math.gcd(512, *group_tokens)
    h = _rmsnorm_rows(xa, xb, attn_norm_g, BF16, row_tile)
    proj = _in_proj(h, w_in.astype(BF16), q_norm_g, k_norm_g, ret_q_w, ret_v_w, attn_w,
                    _tile(n, 1024), _tile(ret_q_w, 1024))
    lg_f = jnp.log1p(-jnp.exp2(-ret_decay_fwd.astype(F32)))
    lg_b = jnp.log1p(-jnp.exp2(-ret_decay_bwd.astype(F32)))
    ret = _retention(proj, lg_f, lg_b, ret_norm_g, n_ret, seq_lens)
    slopes = jnp.exp2(-8.0 * jnp.arange(1, n_attn + 1, dtype=F32) / n_attn)
    att = _dilated_attention(proj, slopes, n_attn, 3, seq_lens, 1024)
    x1 = _out_proj(ret, att, attn_out_g, w_out.astype(BF16), xa, xb, row_tile, _tile(d, 1024))

    h2, logits = _ffn_norm_router(x1, ffn_norm_g, w_router, _tile(n, 512))
    idx_parts, gate_parts, start = [], [], 0
    for tokens in group_tokens:
        cap = EXPERT_CAPACITY * tokens // n_exp
        idx_g, gate_g = _route(logits[start:start + tokens], cap)
        idx_parts.append(idx_g + start)
        gate_parts.append(gate_g)
        start += tokens
    idx = jnp.concatenate(idx_parts, axis=1)
    gate_b = jnp.concatenate(gate_parts, axis=1)
    slots = idx.shape[1]
    idx_flat = idx.reshape(n_exp * slots)
    tm = math.gcd(1024, *[EXPERT_CAPACITY * t // n_exp for t in group_tokens])
    xe = _gather_rows(h2, idx_flat, _tile(tm, 128))
    ye = _moe(xe, w_gate, w_up, w_down, gate_b.reshape(n_exp * slots, LANES), tm, 256)
    return _scatter_add(x1, ye, idx_flat, _tile(tm, 128), slots)


def kernel(x_prompt, x_sample, attn_norm_g, w_in, ret_decay_fwd, ret_decay_bwd, ret_norm_g, q_norm_g, k_norm_g,
           attn_out_g, w_out, ffn_norm_g, w_router, w_gate, w_up, w_down):
    depth = w_in.shape[0]
    d = x_prompt.shape[-1]
    bp, tp, _ = x_prompt.shape
    bs, ts, _ = x_sample.shape
    n_p, n_s = bp * tp, bs * ts
    xa, xb = x_prompt.reshape(n_p, d), x_sample.reshape(n_s, d)
    seq_lens = (tp,) * bp + (ts,) * bs
    for l in range(depth):
        x = _encoder_layer(
            xa, xb, seq_lens, attn_norm_g[l], w_in[l], ret_decay_fwd[l], ret_decay_bwd[l], ret_norm_g[l],
            q_norm_g[l], k_norm_g[l], attn_out_g[l], w_out[l], ffn_norm_g[l], w_router[l],
            w_gate[l], w_up[l], w_down[l])
        xa, xb = x[:n_p], x[n_p:]
    return xa.reshape(bp, tp, d), xb.reshape(bs, ts, d)
```

```python
import functools
import math

import jax
import jax.numpy as jnp
from jax import lax
from jax.experimental import pallas as pl
from jax.experimental.pallas import tpu as pltpu

F32 = jnp.float32
BF16 = jnp.bfloat16
I32 = jnp.int32

RMS_EPS = 1e-6
RET_QK_DIM = 128
RET_V_DIM = 256
HEAD_DIM = 128
RET_CHUNK = 128
DIL_BRANCHES = ((128, 1), (512, 4), (2048, 16))
DIL_RADIUS = 64
EXPERT_CAPACITY = 2
LANES = 128
MASKED = -1e30
MOE_OUT_CHUNK = 512

V7X_VMEM_BYTES = 64 * 1024 * 1024


def _cparams(n_axes, vmem_mb):
    assert vmem_mb * 1024 * 1024 <= V7X_VMEM_BYTES
    return pltpu.CompilerParams(
        dimension_semantics=("arbitrary",) * n_axes,
        vmem_limit_bytes=vmem_mb * 1024 * 1024,
    )


def _any_of(idx, values):
    out = idx == values[0]
    for v in values[1:]:
        out = jnp.logical_or(out, idx == v)
    return out


def _dot(a, b):
    return jnp.dot(a, b, preferred_element_type=F32)


def _dot_nt(a, b):
    return lax.dot_general(a, b, (((1,), (1,)), ((), ())), preferred_element_type=F32)


def _rmsnorm_body(xa_ref, xb_ref, g_ref, o_ref, *, a_tiles):
    def norm(x_ref):
        x = x_ref[...]
        ms = jnp.mean(x * x, axis=-1, keepdims=True)
        o_ref[...] = (x * lax.rsqrt(ms + RMS_EPS) * g_ref[...]).astype(o_ref.dtype)

    @pl.when(pl.program_id(0) < a_tiles)
    def _():
        norm(xa_ref)

    @pl.when(pl.program_id(0) >= a_tiles)
    def _():
        norm(xb_ref)


def _rmsnorm_rows(xa, xb, g, out_dtype, tm):
    (na, d), nb = xa.shape, xb.shape[0]
    assert na % tm == 0 and nb % tm == 0
    a_tiles = na // tm
    return pl.pallas_call(
        functools.partial(_rmsnorm_body, a_tiles=a_tiles),
        grid=((na + nb) // tm,),
        in_specs=[
            pl.BlockSpec((tm, d), lambda i: (jnp.minimum(i, a_tiles - 1), 0)),
            pl.BlockSpec((tm, d), lambda i: (jnp.maximum(i - a_tiles, 0), 0)),
            pl.BlockSpec((1, d), lambda i: (0, 0)),
        ],
        out_specs=pl.BlockSpec((tm, d), lambda i: (i, 0)),
        out_shape=jax.ShapeDtypeStruct((na + nb, d), out_dtype),
        compiler_params=_cparams(1, 56),
        name="rmsnorm",
    )(xa, xb, g.reshape(1, d))


def _head_rmsnorm(x, g_row, scale):
    parts = []
    for c in range(x.shape[1] // HEAD_DIM):
        xs = x[:, c * HEAD_DIM:(c + 1) * HEAD_DIM]
        ms = jnp.mean(xs * xs, axis=-1, keepdims=True)
        parts.append(xs * lax.rsqrt(ms + RMS_EPS) * g_row * scale)
    return parts


def _in_proj_body(h_ref, w_ref, gq_ref, gk_ref, o_ref, *, rk_tiles, aq_tiles, ak_tiles):
    j = pl.program_id(1)
    acc = _dot(h_ref[...], w_ref[...])
    is_rk = jnp.logical_and(j >= rk_tiles[0], j < rk_tiles[1])
    is_aq = jnp.logical_and(j >= aq_tiles[0], j < aq_tiles[1])
    is_ak = jnp.logical_and(j >= ak_tiles[0], j < ak_tiles[1])
    is_plain = jnp.logical_not(jnp.logical_or(is_rk, jnp.logical_or(is_aq, is_ak)))

    @pl.when(is_plain)
    def _():
        o_ref[...] = acc

    @pl.when(is_rk)
    def _():
        o_ref[...] = acc * (RET_QK_DIM ** -0.5)

    @pl.when(is_aq)
    def _():
        for c, part in enumerate(_head_rmsnorm(acc, gq_ref[...], HEAD_DIM ** -0.5)):
            o_ref[:, c * HEAD_DIM:(c + 1) * HEAD_DIM] = part

    @pl.when(is_ak)
    def _():
        for c, part in enumerate(_head_rmsnorm(acc, gk_ref[...], 1.0)):
            o_ref[:, c * HEAD_DIM:(c + 1) * HEAD_DIM] = part


def _in_proj(h, w, gq, gk, ret_q_w, ret_v_w, attn_w, tm, tn):
    n, d = h.shape
    in_w = w.shape[1]
    offs = (ret_q_w, 2 * ret_q_w + 2 * ret_v_w, 2 * ret_q_w + 2 * ret_v_w + attn_w)
    for o in offs + (ret_q_w, attn_w):
        assert o % tn == 0
    rk_tiles = (offs[0] // tn, (offs[0] + ret_q_w) // tn)
    aq_tiles = (offs[1] // tn, (offs[1] + attn_w) // tn)
    ak_tiles = (offs[2] // tn, (offs[2] + attn_w) // tn)
    body = functools.partial(_in_proj_body, rk_tiles=rk_tiles, aq_tiles=aq_tiles, ak_tiles=ak_tiles)
    return pl.pallas_call(
        body,
        grid=(n // tm, in_w // tn),
        in_specs=[
            pl.BlockSpec((tm, d), lambda i, j: (i, 0)),
            pl.BlockSpec((d, tn), lambda i, j: (0, j)),
            pl.BlockSpec((1, HEAD_DIM), lambda i, j: (0, 0)),
            pl.BlockSpec((1, HEAD_DIM), lambda i, j: (0, 0)),
        ],
        out_specs=pl.BlockSpec((tm, tn), lambda i, j: (i, j)),
        out_shape=jax.ShapeDtypeStruct((n, in_w), F32),
        compiler_params=_cparams(2, 56),
        name="in_proj",
    )(h, w, gq.reshape(1, HEAD_DIM), gk.reshape(1, HEAD_DIM))


def _ret_fwd_body(lgf_ref, lgb_ref, q_ref, k_ref, v_ref, o_ref, dmat_ref, state_ref, *, n_heads, start_chunks):
    c = pl.program_id(0)
    C = RET_CHUNK

    @pl.when(c == 0)
    def _():
        row = lax.broadcasted_iota(I32, (C, C), 0)
        col = lax.broadcasted_iota(I32, (C, C), 1)
        diff = (row - col).astype(F32)
        for h in range(n_heads):
            fwd = jnp.where(diff >= 0, jnp.exp(jnp.maximum(diff, 0.0) * lgf_ref[h]), 0.0)
            bwd = jnp.where(diff < 0, jnp.exp(jnp.maximum(-diff, 0.0) * lgb_ref[h]), 0.0)
            dmat_ref[h] = fwd + bwd

    @pl.when(_any_of(c, start_chunks))
    def _():
        state_ref[...] = jnp.zeros_like(state_ref)

    pos = lax.broadcasted_iota(I32, (C, 1), 0).astype(F32)
    for h in range(n_heads):
        lgf = lgf_ref[h]
        q = q_ref[:, h * RET_QK_DIM:(h + 1) * RET_QK_DIM].astype(BF16)
        k = k_ref[:, h * RET_QK_DIM:(h + 1) * RET_QK_DIM]
        v = v_ref[:, h * RET_V_DIM:(h + 1) * RET_V_DIM].astype(BF16)
        scores = _dot_nt(q, k.astype(BF16)) * dmat_ref[h]
        inner = _dot(scores.astype(BF16), v)
        st = state_ref[h]
        cross = _dot(q, st.astype(BF16)) * jnp.exp((pos + 1.0) * lgf)
        o_ref[:, h * RET_V_DIM:(h + 1) * RET_V_DIM] = inner + cross
        kw = k * jnp.exp((C - 1.0 - pos) * lgf)
        kv = _dot(kw.T.astype(BF16), v)
        state_ref[h] = st * jnp.exp(jnp.zeros((1, RET_V_DIM), F32) + C * lgf) + kv


def _ret_bwd_body(lgb_ref, q_ref, k_ref, v_ref, rg_ref, o1_ref, g_ref, o_ref, state_ref, *, n_heads, n_chunks, end_chunks):
    ci = n_chunks - 1 - pl.program_id(0)
    C = RET_CHUNK

    @pl.when(_any_of(ci, end_chunks))
    def _():
        state_ref[...] = jnp.zeros_like(state_ref)

    pos = lax.broadcasted_iota(I32, (C, 1), 0).astype(F32)
    for h in range(n_heads):
        lgb = lgb_ref[h]
        vs = slice(h * RET_V_DIM, (h + 1) * RET_V_DIM)
        q = q_ref[:, h * RET_QK_DIM:(h + 1) * RET_QK_DIM].astype(BF16)
        k = k_ref[:, h * RET_QK_DIM:(h + 1) * RET_QK_DIM]
        v = v_ref[:, vs].astype(BF16)
        st = state_ref[h]
        cross = _dot(q, st.astype(BF16)) * jnp.exp((C - pos) * lgb)
        o = o1_ref[:, vs] + cross
        ms = jnp.mean(o * o, axis=-1, keepdims=True)
        r = o * lax.rsqrt(ms + RMS_EPS) * g_ref[:, vs]
        gate = rg_ref[:, vs]
        r = r * (gate / (1.0 + jnp.exp(-gate)))
        o_ref[:, vs] = r.astype(o_ref.dtype)
        kw = k * jnp.exp(pos * lgb)
        kv = _dot(kw.T.astype(BF16), v)
        state_ref[h] = st * jnp.exp(jnp.zeros((1, RET_V_DIM), F32) + C * lgb) + kv


def _retention(proj, lg_f, lg_b, ret_norm_g, n_heads, seq_lens):
    n = proj.shape[0]
    C = RET_CHUNK
    qw = n_heads * RET_QK_DIM
    vw = n_heads * RET_V_DIM
    n_chunks = n // C
    starts, ends, acc = [], [], 0
    for t in seq_lens:
        starts.append(acc // C)
        acc += t
        ends.append(acc // C - 1)
    smem = pl.BlockSpec(memory_space=pltpu.SMEM)
    o1 = pl.pallas_call(
        functools.partial(_ret_fwd_body, n_heads=n_heads, start_chunks=tuple(starts)),
        grid=(n_chunks,),
        in_specs=[
            smem, smem,
            pl.BlockSpec((C, qw), lambda c: (c, 0)),
            pl.BlockSpec((C, qw), lambda c: (c, 1)),
            pl.BlockSpec((C, vw), lambda c: (c, 1)),
        ],
        out_specs=pl.BlockSpec((C, vw), lambda c: (c, 0)),
        out_shape=jax.ShapeDtypeStruct((n, vw), F32),
        scratch_shapes=[pltpu.VMEM((n_heads, C, C), F32), pltpu.VMEM((n_heads, RET_QK_DIM, RET_V_DIM), F32)],
        compiler_params=_cparams(1, 32),
        name="retention_fwd",
    )(lg_f, lg_b, proj, proj, proj)
    last = n_chunks - 1
    return pl.pallas_call(
        functools.partial(_ret_bwd_body, n_heads=n_heads, n_chunks=n_chunks, end_chunks=tuple(ends)),
        grid=(n_chunks,),
        in_specs=[
            smem,
            pl.BlockSpec((C, qw), lambda c: (last - c, 0)),
            pl.BlockSpec((C, qw), lambda c: (last - c, 1)),
            pl.BlockSpec((C, vw), lambda c: (last - c, 1)),
            pl.BlockSpec((C, vw), lambda c: (last - c, 2)),
            pl.BlockSpec((C, vw), lambda c: (last - c, 0)),
            pl.BlockSpec((1, vw), lambda c: (0, 0)),
        ],
        out_specs=pl.BlockSpec((C, vw), lambda c: (last - c, 0)),
        out_shape=jax.ShapeDtypeStruct((n, vw), BF16),
        scratch_shapes=[pltpu.VMEM((n_heads, RET_QK_DIM, RET_V_DIM), F32)],
        compiler_params=_cparams(1, 32),
        name="retention_bwd",
    )(lg_b, proj, proj, proj, proj, o1, ret_norm_g.reshape(1, vw))


def _deinterleave(x, dil):
    return x if dil == 1 else pltpu.einshape("(ir)c->(ri)c", x, r=dil)


def _interleave(x, dil):
    return x if dil == 1 else pltpu.einshape("(ri)c->(ir)c", x, r=dil)


def _attn_unit_rows(tq, dil):
    return min(256, tq // dil)


def _attn_body(slopes_ref, q_ref, k0_ref, kn_ref, v0_ref, vn_ref, o_ref, kbuf, vbuf, qbuf, *bias_refs,
               tq, start_tiles, end_tiles):
    h, i = pl.program_id(0), pl.program_id(1)
    rad = DIL_RADIUS
    unit_rows = sorted({_attn_unit_rows(tq, dil) for _, dil in DIL_BRANCHES})
    bias_of = dict(zip(unit_rows, bias_refs))

    @pl.when(jnp.logical_and(h == 0, i == 0))
    def _():
        for R, ref in bias_of.items():
            ii = lax.broadcasted_iota(I32, (R, R + 2 * rad), 0)
            jj = lax.broadcasted_iota(I32, (R, R + 2 * rad), 1)
            dist = jnp.abs(jj - rad - ii)
            ref[...] = jnp.where(dist <= rad, -dist.astype(F32), MASKED)

    def fill(slot, k_ref, v_ref):
        kx, vx = k_ref[...], v_ref[...]
        for bi, (_, dil) in enumerate(DIL_BRANCHES):
            kbuf[slot, bi] = _deinterleave(kx, dil).astype(BF16)
            vbuf[slot, bi] = _deinterleave(vx, dil).astype(BF16)

    @pl.when(i == 0)
    def _():
        fill(0, k0_ref, v0_ref)
        kbuf[2] = jnp.zeros_like(kbuf[2])
        vbuf[2] = jnp.zeros_like(vbuf[2])

    cur, nxt, prv = i % 3, (i + 1) % 3, (i + 2) % 3
    fill(nxt, kn_ref, vn_ref)
    qx = q_ref[...]
    for bi, (_, dil) in enumerate(DIL_BRANCHES):
        qbuf[bi] = _deinterleave(qx, dil).astype(BF16)

    slope = slopes_ref[h]
    no_prev = jnp.where(_any_of(i, start_tiles), 1.0, 0.0)
    no_next = jnp.where(_any_of(i, end_tiles), 1.0, 0.0)

    groups = {}
    for bi, (_, dil) in enumerate(DIL_BRANCHES):
        sub = tq // dil
        R = _attn_unit_rows(tq, dil)
        for residue in range(dil):
            for u in range(sub // R):
                groups.setdefault(R, []).append((bi, dil, residue * sub, sub, u * R, u == 0, u == sub // R - 1))

    def window(buf, bi, base, sub, off, first, final, R):
        parts = []
        if first:
            parts.append(buf[prv, bi, base + sub - rad:base + sub, :])
        parts.append(buf[cur, bi, base + off - (0 if first else rad):base + off + R + (0 if final else rad), :])
        if final:
            parts.append(buf[nxt, bi, base:base + rad, :])
        return jnp.concatenate(parts, axis=0) if len(parts) > 1 else parts[0]

    outs, lses = [None] * len(DIL_BRANCHES), [None] * len(DIL_BRANCHES)
    for R, units in groups.items():
        B, W = len(units), R + 2 * rad
        q = jnp.stack([qbuf[bi, base + off:base + off + R, :] for bi, _, base, _, off, _, _ in units])
        k = jnp.stack([window(kbuf, bi, base, sub, off, fi, fa, R) for bi, _, base, sub, off, fi, fa in units])
        v = jnp.stack([window(vbuf, bi, base, sub, off, fi, fa, R) for bi, _, base, sub, off, fi, fa in units])
        b_idx = lax.broadcasted_iota(I32, (B, 1, 1), 0)
        col = lax.broadcasted_iota(I32, (1, 1, W), 2)
        dil_b = jnp.zeros((B, 1, 1), F32)
        for dil in sorted({u[1] for u in units}):
            dil_b = dil_b + jnp.where(_any_of(b_idx, [b for b, u in enumerate(units) if u[1] == dil]), float(dil), 0.0)
        first_b = jnp.where(_any_of(b_idx, [b for b, u in enumerate(units) if u[5]]), no_prev, 0.0)
        final_b = jnp.where(_any_of(b_idx, [b for b, u in enumerate(units) if u[6]]), no_next, 0.0)
        s = lax.dot_general(q, k, (((2,), (2,)), ((0,), (0,))), preferred_element_type=F32)
        s = s + (slope * dil_b) * bias_of[R][...][None]
        s = s + first_b * jnp.where(col < rad, MASKED, 0.0)
        s = s + final_b * jnp.where(col >= R + rad, MASKED, 0.0)
        m = jnp.max(s, axis=-1, keepdims=True)
        p = jnp.exp(s - m)
        l = jnp.sum(p, axis=-1, keepdims=True)
        o = lax.dot_general(p.astype(BF16), v, (((2,), (1,)), ((0,), (0,))), preferred_element_type=F32) / l
        lse = jnp.broadcast_to(m + jnp.log(l), (B, R, LANES))
        for bi, (_, dil) in enumerate(DIL_BRANCHES):
            mine = [b for b, u in enumerate(units) if u[0] == bi]
            if mine:
                lo, hi = mine[0], mine[-1] + 1
                outs[bi] = _interleave(o[lo:hi].reshape((hi - lo) * R, HEAD_DIM), dil)
                lses[bi] = _interleave(lse[lo:hi].reshape((hi - lo) * R, LANES), dil)
    m = functools.reduce(jnp.maximum, lses)
    num = jnp.zeros((tq, LANES), F32)
    den = jnp.zeros((tq, LANES), F32)
    for o, lse in zip(outs, lses):
        w = jnp.exp(lse - m)
        num = num + w * o
        den = den + w
    o_ref[...] = num / den


def _dilated_attention(proj, slopes, n_heads, q_blk, seq_lens, tq):
    n = proj.shape[0]
    nt = n // tq
    starts, ends, acc = [], [], 0
    for t in seq_lens:
        assert t % tq == 0
        starts.append(acc // tq)
        acc += t
        ends.append(acc // tq - 1)
    qc, kc, vc = q_blk * n_heads, (q_blk + 1) * n_heads, (q_blk + 2) * n_heads
    blk = (tq, HEAD_DIM)
    nxt = lambda i: jnp.minimum(i + 1, nt - 1)
    n_br = len(DIL_BRANCHES)
    unit_rows = sorted({_attn_unit_rows(tq, dil) for _, dil in DIL_BRANCHES})
    body = functools.partial(_attn_body, tq=tq, start_tiles=tuple(starts), end_tiles=tuple(ends))
    return pl.pallas_call(
        body,
        grid=(n_heads, nt),
        in_specs=[
            pl.BlockSpec(memory_space=pltpu.SMEM),
            pl.BlockSpec(blk, lambda h, i: (i, qc + h)),
            pl.BlockSpec(blk, lambda h, i: (0, kc + h)),
            pl.BlockSpec(blk, lambda h, i: (nxt(i), kc + h)),
            pl.BlockSpec(blk, lambda h, i: (0, vc + h)),
            pl.BlockSpec(blk, lambda h, i: (nxt(i), vc + h)),
        ],
        out_specs=pl.BlockSpec(blk, lambda h, i: (i, h)),
        out_shape=jax.ShapeDtypeStruct((n, n_heads * HEAD_DIM), F32),
        scratch_shapes=[
            pltpu.VMEM((3, n_br, tq, HEAD_DIM), BF16),
            pltpu.VMEM((3, n_br, tq, HEAD_DIM), BF16),
            pltpu.VMEM((n_br, tq, HEAD_DIM), BF16),
        ] + [pltpu.VMEM((r, r + 2 * DIL_RADIUS), F32) for r in unit_rows],
        compiler_params=_cparams(2, 32),
        name="dilated_attention",
    )(slopes, proj, proj, proj, proj, proj)


def _out_proj_body(ret_ref, att_ref, g_ref, w_ref, xa_ref, xb_ref, oa_ref, ob_ref, cat_ref, *, ret_w, a_tiles):
    i = pl.program_id(0)

    @pl.when(pl.program_id(1) == 0)
    def _():
        cat_ref[:, :ret_w] = ret_ref[...]
        a = att_ref[...]
        ms = jnp.mean(a * a, axis=-1, keepdims=True)
        cat_ref[:, ret_w:] = (a * lax.rsqrt(ms + RMS_EPS) * g_ref[...]).astype(BF16)

    mixed = _dot(cat_ref[...], w_ref[...])

    @pl.when(i < a_tiles)
    def _():
        oa_ref[...] = xa_ref[...] + mixed

    @pl.when(i >= a_tiles)
    def _():
        ob_ref[...] = xb_ref[...] + mixed


def _out_proj(ret, att, attn_out_g, w_out, xa, xb, tm, tn):
    (na, d), nb = xa.shape, xb.shape[0]
    assert na % tm == 0 and nb % tm == 0
    a_tiles = na // tm
    last_j = d // tn - 1
    ret_w, attn_w = ret.shape[1], att.shape[1]
    in_a = lambda i, j: (jnp.minimum(i, a_tiles - 1), jnp.where(i < a_tiles, j, last_j))
    in_b = lambda i, j: (jnp.maximum(i - a_tiles, 0), jnp.where(i >= a_tiles, j, 0))
    return pl.pallas_call(
        functools.partial(_out_proj_body, ret_w=ret_w, a_tiles=a_tiles),
        grid=((na + nb) // tm, d // tn),
        in_specs=[
            pl.BlockSpec((tm, ret_w), lambda i, j: (i, 0)),
            pl.BlockSpec((tm, attn_w), lambda i, j: (i, 0)),
            pl.BlockSpec((1, attn_w), lambda i, j: (0, 0)),
            pl.BlockSpec((ret_w + attn_w, tn), lambda i, j: (0, j)),
            pl.BlockSpec((tm, tn), in_a),
            pl.BlockSpec((tm, tn), in_b),
        ],
        out_specs=[pl.BlockSpec((tm, tn), in_a), pl.BlockSpec((tm, tn), in_b)],
        out_shape=[jax.ShapeDtypeStruct((na, d), F32), jax.ShapeDtypeStruct((nb, d), F32)],
        scratch_shapes=[pltpu.VMEM((tm, ret_w + attn_w), BF16)],
        compiler_params=_cparams(2, 56),
        name="out_proj",
    )(ret, att, attn_out_g.reshape(1, attn_w), w_out, xa, xb)


def _ffn_rmsnorm(x, g_row):
    ms = jnp.mean(x * x, axis=-1, keepdims=True)
    return x * lax.rsqrt(ms + RMS_EPS) * g_row


def _dot_split(a, b):
    a_hi, b_hi = a.astype(BF16), b.astype(BF16)
    a_lo = (a - a_hi.astype(F32)).astype(BF16)
    b_lo = (b - b_hi.astype(F32)).astype(BF16)
    return _dot(a_hi, b_hi) + (_dot(a_hi, b_lo) + _dot(a_lo, b_hi))


def _ffn_norm_router_body(x_ref, g_ref, wr_ref, lg_ref):
    h = _ffn_rmsnorm(x_ref[...], g_ref[...])
    lg_ref[...] = _dot_split(h, wr_ref[...])


def _ffn_norm_router(x1, g, w_router, tm):
    n, d = x1.shape
    e = w_router.shape[1]
    return pl.pallas_call(
        _ffn_norm_router_body,
        grid=(n // tm,),
        in_specs=[
            pl.BlockSpec((tm, d), lambda i: (i, 0)),
            pl.BlockSpec((1, d), lambda i: (0, 0)),
            pl.BlockSpec((d, e), lambda i: (0, 0)),
        ],
        out_specs=pl.BlockSpec((tm, e), lambda i: (i, 0)),
        out_shape=jax.ShapeDtypeStruct((n, e), F32),
        compiler_params=_cparams(1, 48),
        name="ffn_norm_router",
    )(x1, g.reshape(1, d), w_router)


def _exclusive_cumsum_tokens(x, upper, lower):
    e, nb, _ = x.shape
    within = _dot(x.reshape(e * nb, LANES).astype(BF16), upper).reshape(e, nb, LANES)
    tot = jnp.sum(x, axis=2, keepdims=True)
    tot_b = jnp.concatenate([jnp.broadcast_to(tot[ei], (nb, LANES)) for ei in range(e)], axis=1)
    off_b = _dot(lower, tot_b.astype(BF16))
    off = jnp.stack([off_b[:, ei * LANES:(ei + 1) * LANES] for ei in range(e)], axis=0)
    return within + off, off


def _route_select_body(lt_ref, aff_ref, sel_ref, pos_ref, off_ref, *, cap):
    lt = lt_ref[...]
    e, nb, _ = lt.shape
    m = jnp.max(lt, axis=0, keepdims=True)
    ex = jnp.exp(lt - m)
    aff = ex / jnp.sum(ex, axis=0, keepdims=True)
    aff_ref[...] = aff
    bits = pltpu.bitcast(aff, I32)

    def count(mask):
        c = jnp.sum(jnp.where(mask, 1.0, 0.0), axis=2, keepdims=True)
        return jnp.sum(c, axis=1, keepdims=True)

    thr = jnp.zeros((e, 1, 1), I32)
    for b in range(30, -1, -1):
        cand = thr | (1 << b)
        thr = jnp.where(count(bits >= cand) >= cap, cand, thr)

    ii = lax.broadcasted_iota(I32, (LANES, LANES), 0)
    jj = lax.broadcasted_iota(I32, (LANES, LANES), 1)
    upper = jnp.where(ii < jj, 1.0, 0.0).astype(BF16)
    bi = lax.broadcasted_iota(I32, (nb, nb), 0)
    bj = lax.broadcasted_iota(I32, (nb, nb), 1)
    lower = jnp.where(bj < bi, 1.0, 0.0).astype(BF16)

    gt = bits > thr
    eq = bits == thr
    need = cap - count(gt)
    eq_rank, _ = _exclusive_cumsum_tokens(jnp.where(eq, 1.0, 0.0), upper, lower)
    sel = jnp.where(gt | (eq & (eq_rank < need)), 1.0, 0.0)
    pos, off = _exclusive_cumsum_tokens(sel, upper, lower)
    sel_ref[...] = sel
    pos_ref[...] = pos
    off_ref[...] = off.astype(I32)


def _route_compact_body(off_ref, sel_ref, pos_ref, aff_ref, idx_ref, gate_ref, acc_i, acc_g, *, nb, n_tiles):
    e = pl.program_id(0)
    acc_i[...] = jnp.zeros_like(acc_i)
    acc_g[...] = jnp.zeros_like(acc_g)
    lane = lax.broadcasted_iota(I32, (1, LANES), 1).astype(F32)
    slot = lax.broadcasted_iota(I32, (LANES, 1), 0).astype(F32)

    def block(b, carry):
        first = off_ref[e, b] // LANES
        pos = pos_ref[0, pl.ds(b, 1), :]
        chosen = sel_ref[0, pl.ds(b, 1), :] > 0.0
        aff = aff_ref[0, pl.ds(b, 1), :]
        tok = lane + lax.convert_element_type(b * LANES, F32)
        for dj in range(2):
            t = first + dj
            match = chosen & (pos == slot + lax.convert_element_type(t * LANES, F32))
            acc_i[t] = acc_i[t] + jnp.where(match, tok, 0.0)
            acc_g[t] = acc_g[t] + jnp.where(match, aff, 0.0)
        return carry

    lax.fori_loop(0, nb, block, 0)
    for t in range(n_tiles):
        rows = slice(t * LANES, (t + 1) * LANES)
        idx_ref[0, rows, :] = jnp.broadcast_to(jnp.sum(acc_i[t], axis=1, keepdims=True), (LANES, LANES))
        gate_ref[0, rows, :] = jnp.broadcast_to(jnp.sum(acc_g[t], axis=1, keepdims=True), (LANES, LANES))


def _route(logits, cap):
    n, e = logits.shape
    nb = n // LANES
    n_tiles = cap // LANES
    lt = logits.T.reshape(e, nb, LANES)
    shp = jax.ShapeDtypeStruct((e, nb, LANES), F32)
    aff, sel, pos, off = pl.pallas_call(
        functools.partial(_route_select_body, cap=cap),
        out_shape=[shp, shp, shp, jax.ShapeDtypeStruct((e, nb, LANES), I32)],
        compiler_params=pltpu.CompilerParams(vmem_limit_bytes=48 * 1024 * 1024),
        name="route_select",
    )(lt)
    blk = pl.BlockSpec((1, nb, LANES), lambda ei, off_ref: (ei, 0, 0))
    oblk = pl.BlockSpec((1, cap, LANES), lambda ei, off_ref: (ei, 0, 0))
    oshp = jax.ShapeDtypeStruct((e, cap, LANES), F32)
    idx_b, gate_b = pl.pallas_call(
        functools.partial(_route_compact_body, nb=nb, n_tiles=n_tiles),
        grid_spec=pltpu.PrefetchScalarGridSpec(
            num_scalar_prefetch=1,
            grid=(e,),
            in_specs=[blk, blk, blk],
            out_specs=[oblk, oblk],
            scratch_shapes=[pltpu.VMEM((n_tiles + 2, LANES, LANES), F32)] * 2,
        ),
        out_shape=[oshp, oshp],
        compiler_params=_cparams(1, 32),
        name="route_compact",
    )(off[:, :, 0], sel, pos, aff)
    return idx_b[:, :, 0].astype(I32), gate_b


def _group_rows(step, rows, chunks, a_chunks):
    return (step % chunks) < a_chunks


def _gather_body(idx_ref, xa_hbm, xb_hbm, g_ref, o_ref, buf, sem, *, rows, n_steps, chunks, a_chunks, a_rows):
    s = pl.program_id(0)
    slot = s % 2

    def issue(step, into):
        def copy_from(src_hbm, shift):
            def one(r, carry):
                src = src_hbm.at[pl.ds(idx_ref[step * rows + r] - shift, 1), :]
                pltpu.make_async_copy(src, buf.at[into, pl.ds(r, 1), :], sem.at[into]).start()
                return carry

            lax.fori_loop(0, rows, one, 0, unroll=8)

        in_a = _group_rows(step, rows, chunks, a_chunks)

        @pl.when(in_a)
        def _():
            copy_from(xa_hbm, 0)

        @pl.when(jnp.logical_not(in_a))
        def _():
            copy_from(xb_hbm, a_rows)

    @pl.when(s == 0)
    def _():
        issue(0, 0)

    @pl.when(s + 1 < n_steps)
    def _():
        issue(s + 1, 1 - slot)

    pltpu.make_async_copy(xa_hbm.at[pl.ds(0, rows), :], buf.at[slot], sem.at[slot]).wait()
    o_ref[...] = _ffn_rmsnorm(buf[slot], g_ref[...]).astype(o_ref.dtype)


def _gather_norm_rows(xa, xb, g, idx_flat, rows, slots_per_expert, a_slots):
    (na, d), total = xa.shape, idx_flat.shape[0]
    n_steps = total // rows
    assert slots_per_expert % rows == 0 and a_slots % rows == 0
    body = functools.partial(_gather_body, rows=rows, n_steps=n_steps, chunks=slots_per_expert // rows,
                             a_chunks=a_slots // rows, a_rows=na)
    return pl.pallas_call(
        body,
        grid_spec=pltpu.PrefetchScalarGridSpec(
            num_scalar_prefetch=1,
            grid=(n_steps,),
            in_specs=[pl.BlockSpec(memory_space=pl.ANY), pl.BlockSpec(memory_space=pl.ANY),
                      pl.BlockSpec((1, d), lambda s, idx: (0, 0))],
            out_specs=pl.BlockSpec((rows, d), lambda s, idx: (s, 0)),
            scratch_shapes=[pltpu.VMEM((2, rows, d), F32), pltpu.SemaphoreType.DMA((2,))],
        ),
        out_shape=jax.ShapeDtypeStruct((total, d), BF16),
        compiler_params=_cparams(1, 32),
        name="gather_rows",
    )(idx_flat, xa, xb, g.reshape(1, d))


def _moe_body(x_ref, wg_ref, wu_ref, wd_ref, gate_ref, o_ref, *, nf):
    f = pl.program_id(2)

    @pl.when(f == 0)
    def _():
        o_ref[...] = jnp.zeros_like(o_ref)

    x = x_ref[...]
    g = _dot(x, wg_ref[...].astype(BF16))
    u = _dot(x, wu_ref[...].astype(BF16))
    hid = (g / (1.0 + jnp.exp(-g)) * u).astype(BF16)
    for c in range(o_ref.shape[1] // MOE_OUT_CHUNK):
        cols = slice(c * MOE_OUT_CHUNK, (c + 1) * MOE_OUT_CHUNK)
        o_ref[:, cols] += _dot(hid, wd_ref[:, cols].astype(BF16))

    @pl.when(f == nf - 1)
    def _():
        o_ref[...] = o_ref[...] * gate_ref[:, 0:1]


def _moe(xe, w_gate, w_up, w_down, gate_b, tm, tf):
    e, d, fdim = w_gate.shape
    total = xe.shape[0]
    tiles = total // e // tm
    nf = fdim // tf
    assert nf >= 2 and fdim % tf == 0
    row = lambda ei, s, f: (ei * tiles + s, 0)
    return pl.pallas_call(
        functools.partial(_moe_body, nf=nf),
        grid=(e, tiles, nf),
        in_specs=[
            pl.BlockSpec((tm, d), row, pipeline_mode=pl.Buffered(1)),
            pl.BlockSpec((None, d, tf), lambda ei, s, f: (ei, 0, f)),
            pl.BlockSpec((None, d, tf), lambda ei, s, f: (ei, 0, f)),
            pl.BlockSpec((None, tf, d), lambda ei, s, f: (ei, f, 0)),
            pl.BlockSpec((tm, LANES), row),
        ],
        out_specs=pl.BlockSpec((tm, d), row, pipeline_mode=pl.Buffered(1)),
        out_shape=jax.ShapeDtypeStruct((total, d), F32),
        compiler_params=_cparams(3, 60),
        name="expert_swiglu",
    )(xe, w_gate, w_up, w_down, gate_b)


def _scatter_body(idx_ref, ye_ref, ya_in, yb_in, ya_hbm, yb_hbm, buf, sem_in, sem_out,
                  *, rows, n_steps, chunks, a_chunks, a_rows):
    del ya_in, yb_in
    s = pl.program_id(0)
    c = s % chunks
    slot = s % 3

    def per_group(step, fn):
        in_a = _group_rows(step, rows, chunks, a_chunks)

        @pl.when(in_a)
        def _():
            fn(ya_hbm, 0)

        @pl.when(jnp.logical_not(in_a))
        def _():
            fn(yb_hbm, a_rows)

    def fetch(step):
        into = step % 3

        def from_group(y_hbm, shift):
            def one(r, carry):
                src = y_hbm.at[pl.ds(idx_ref[step * rows + r] - shift, 1), :]
                pltpu.make_async_copy(src, buf.at[into, pl.ds(r, 1), :], sem_in.at[into]).start()
                return carry

            lax.fori_loop(0, rows, one, 0, unroll=8)

        per_group(step, from_group)

    def write(step):
        out_of = step % 3

        def to_group(y_hbm, shift):
            def one(r, carry):
                dst = y_hbm.at[pl.ds(idx_ref[step * rows + r] - shift, 1), :]
                pltpu.make_async_copy(buf.at[out_of, pl.ds(r, 1), :], dst, sem_out.at[out_of]).start()
                return carry

            lax.fori_loop(0, rows, one, 0, unroll=8)

        per_group(step, to_group)

    def wait_write(ring_slot):
        pltpu.make_async_copy(buf.at[ring_slot], ya_hbm.at[pl.ds(0, rows), :], sem_out.at[ring_slot]).wait()

    @pl.when(s == 0)
    def _():
        fetch(0)

    pltpu.make_async_copy(ya_hbm.at[pl.ds(0, rows), :], buf.at[slot], sem_in.at[slot]).wait()
    buf[slot] = buf[slot] + ye_ref[...]
    write(s)

    @pl.when(c == 0)
    def _():
        fetch(s + 1)
        fetch(s + 2)

    @pl.when(jnp.logical_and(c >= 1, c <= chunks - 3))
    def _():
        wait_write((s + 2) % 3)
        fetch(s + 2)

    @pl.when(c == chunks - 1)
    def _():
        wait_write((s + 1) % 3)
        wait_write((s + 2) % 3)
        wait_write(slot)

        @pl.when(s + 1 < n_steps)
        def _():
            fetch(s + 1)


def _scatter_add(ya, yb, ye, idx_flat, rows, slots_per_expert, a_slots):
    (na, d), nb = ya.shape, yb.shape[0]
    total = ye.shape[0]
    n_steps = total // rows
    chunks = slots_per_expert // rows
    assert chunks >= 3 and slots_per_expert % rows == 0 and a_slots % rows == 0
    body = functools.partial(_scatter_body, rows=rows, n_steps=n_steps, chunks=chunks,
                             a_chunks=a_slots // rows, a_rows=na)
    anyspace = pl.BlockSpec(memory_space=pl.ANY)
    return pl.pallas_call(
        body,
        grid_spec=pltpu.PrefetchScalarGridSpec(
            num_scalar_prefetch=1,
            grid=(n_steps,),
            in_specs=[pl.BlockSpec((rows, d), lambda s, idx: (s, 0)), anyspace, anyspace],
            out_specs=[anyspace, anyspace],
            scratch_shapes=[pltpu.VMEM((3, rows, d), F32), pltpu.SemaphoreType.DMA((3,)),
                            pltpu.SemaphoreType.DMA((3,))],
        ),
        out_shape=[jax.ShapeDtypeStruct((na, d), F32), jax.ShapeDtypeStruct((nb, d), F32)],
        input_output_aliases={2: 0, 3: 1},
        compiler_params=_cparams(1, 32),
        name="scatter_add",
    )(idx_flat, ye, ya, yb)


def _tile(n, pref):
    t = min(n, pref)
    assert n % t == 0
    return t


def _encoder_layer(xa, xb, seq_lens, attn_norm_g, w_in, ret_decay_fwd, ret_decay_bwd, ret_norm_g,
                   q_norm_g, k_norm_g, attn_out_g, w_out, ffn_norm_g, w_router, w_gate, w_up, w_down):
    group_tokens = (xa.shape[0], xb.shape[0])
    n, d = sum(group_tokens), xa.shape[1]
    n_ret = ret_decay_fwd.shape[0]
    ret_q_w, ret_v_w = n_ret * RET_QK_DIM, n_ret * RET_V_DIM
    attn_w = attn_out_g.shape[0]
    n_attn = attn_w // HEAD_DIM
    assert ret_v_w == attn_w and w_in.shape[1] == 2 * ret_q_w + 2 * ret_v_w + 3 * attn_w
    n_exp = w_router.shape[1]

    row_tile = math.gcd(512, *group_tokens)
    h = _rmsnorm_rows(xa, xb, attn_norm_g, BF16, row_tile)
    proj = _in_proj(h, w_in.astype(BF16), q_norm_g, k_norm_g, ret_q_w, ret_v_w, attn_w,
                    _tile(n, 1024), _tile(ret_q_w, 1024))
    lg_f = jnp.log1p(-jnp.exp2(-ret_decay_fwd.astype(F32)))
    lg_b = jnp.log1p(-jnp.exp2(-ret_decay_bwd.astype(F32)))
    ret = _retention(proj, lg_f, lg_b, ret_norm_g, n_ret, seq_lens)
    slopes = jnp.exp2(-8.0 * jnp.arange(1, n_attn + 1, dtype=F32) / n_attn)
    att = _dilated_attention(proj, slopes, n_attn, 3, seq_lens, 1024)
    x1a, x1b = _out_proj(ret, att, attn_out_g, w_out.astype(BF16), xa, xb, row_tile, _tile(d, 1024))

    idx_parts, gate_parts, start = [], [], 0
    for x1g in (x1a, x1b):
        tokens = x1g.shape[0]
        cap = EXPERT_CAPACITY * tokens // n_exp
        logits = _ffn_norm_router(x1g, ffn_norm_g, w_router, row_tile)
        idx_g, gate_g = _route(logits, cap)
        idx_parts.append(idx_g + start)
        gate_parts.append(gate_g)
        start += tokens
    idx = jnp.concatenate(idx_parts, axis=1)
    gate_b = jnp.concatenate(gate_parts, axis=1)
    slots, a_slots = idx.shape[1], idx_parts[0].shape[1]
    idx_flat = idx.reshape(n_exp * slots)
    tm = math.gcd(1024, *[EXPERT_CAPACITY * t // n_exp for t in group_tokens])
    rows = _tile(tm, 128)
    xe = _gather_norm_rows(x1a, x1b, ffn_norm_g, idx_flat, rows, slots, a_slots)
    ye = _moe(xe, w_gate, w_up, w_down, gate_b.reshape(n_exp * slots, LANES), tm, 256)
    return _scatter_add(x1a, x1b, ye, idx_flat, rows, slots, a_slots)


def kernel(x_prompt, x_sample, attn_norm_g, w_in, ret_decay_fwd, ret_decay_bwd, ret_norm_g, q_norm_g, k_norm_g,
           attn_out_g, w_out, ffn_norm_g, w_router, w_gate, w_up, w_down):
    depth = w_in.shape[0]
    d = x_prompt.shape[-1]
    bp, tp, _ = x_prompt.shape
    bs, ts, _ = x_sample.shape
    n_p, n_s = bp * tp, bs * ts
    xa, xb = x_prompt.reshape(n_p, d), x_sample.reshape(n_s, d)
    seq_lens = (tp,) * bp + (ts,) * bs
    for l in range(depth):
        xa, xb = _encoder_layer(
            xa, xb, seq_lens, attn_norm_g[l], w_in[l], ret_decay_fwd[l], ret_decay_bwd[l], ret_norm_g[l],
            q_norm_g[l], k_norm_g[l], attn_out_g[l], w_out[l], ffn_norm_g[l], w_router[l],
            w_gate[l], w_up[l], w_down[l])
    return xa.reshape(bp, tp, d), xb.reshape(bs, ts, d)
```

```python
import functools
import math

import jax
import jax.numpy as jnp
from jax import lax
from jax.experimental import pallas as pl
from jax.experimental.pallas import tpu as pltpu

F32 = jnp.float32
BF16 = jnp.bfloat16
I32 = jnp.int32

RMS_EPS = 1e-6
RET_QK_DIM = 128
RET_V_DIM = 256
HEAD_DIM = 128
RET_CHUNK = 128
DIL_BRANCHES = ((128, 1), (512, 4), (2048, 16))
DIL_RADIUS = 64
EXPERT_CAPACITY = 2
LANES = 128
MASKED = -1e30
MOE_OUT_CHUNK = 512
GATHER_NORM_ROWS = 16

V7X_VMEM_BYTES = 64 * 1024 * 1024


def _cparams(n_axes, vmem_mb):
    assert vmem_mb * 1024 * 1024 <= V7X_VMEM_BYTES
    return pltpu.CompilerParams(
        dimension_semantics=("arbitrary",) * n_axes,
        vmem_limit_bytes=vmem_mb * 1024 * 1024,
    )


def _any_of(idx, values):
    out = idx == values[0]
    for v in values[1:]:
        out = jnp.logical_or(out, idx == v)
    return out


def _dot(a, b):
    return jnp.dot(a, b, preferred_element_type=F32)


def _dot_nt(a, b):
    return lax.dot_general(a, b, (((1,), (1,)), ((), ())), preferred_element_type=F32)


def _rmsnorm_body(xa_ref, xb_ref, g_ref, o_ref, *, a_tiles):
    def norm(x_ref):
        x = x_ref[...]
        ms = jnp.mean(x * x, axis=-1, keepdims=True)
        o_ref[...] = (x * lax.rsqrt(ms + RMS_EPS) * g_ref[...]).astype(o_ref.dtype)

    @pl.when(pl.program_id(0) < a_tiles)
    def _():
        norm(xa_ref)

    @pl.when(pl.program_id(0) >= a_tiles)
    def _():
        norm(xb_ref)


def _rmsnorm_rows(xa, xb, g, out_dtype, tm):
    (na, d), nb = xa.shape, xb.shape[0]
    assert na % tm == 0 and nb % tm == 0
    a_tiles = na // tm
    return pl.pallas_call(
        functools.partial(_rmsnorm_body, a_tiles=a_tiles),
        grid=((na + nb) // tm,),
        in_specs=[
            pl.BlockSpec((tm, d), lambda i: (jnp.minimum(i, a_tiles - 1), 0)),
            pl.BlockSpec((tm, d), lambda i: (jnp.maximum(i - a_tiles, 0), 0)),
            pl.BlockSpec((1, d), lambda i: (0, 0)),
        ],
        out_specs=pl.BlockSpec((tm, d), lambda i: (i, 0)),
        out_shape=jax.ShapeDtypeStruct((na + nb, d), out_dtype),
        compiler_params=_cparams(1, 56),
        name="rmsnorm",
    )(xa, xb, g.reshape(1, d))


def _head_rmsnorm(x, g_row, scale):
    parts = []
    for c in range(x.shape[1] // HEAD_DIM):
        xs = x[:, c * HEAD_DIM:(c + 1) * HEAD_DIM]
        ms = jnp.mean(xs * xs, axis=-1, keepdims=True)
        parts.append(xs * lax.rsqrt(ms + RMS_EPS) * g_row * scale)
    return parts


def _in_proj_body(h_ref, w_ref, gq_ref, gk_ref, o_ref, *, rk_tiles, aq_tiles, ak_tiles):
    j = pl.program_id(1)
    o_ref[...] = _dot(h_ref[...], w_ref[...])
    is_rk = jnp.logical_and(j >= rk_tiles[0], j < rk_tiles[1])
    is_aq = jnp.logical_and(j >= aq_tiles[0], j < aq_tiles[1])
    is_ak = jnp.logical_and(j >= ak_tiles[0], j < ak_tiles[1])

    @pl.when(is_rk)
    def _():
        o_ref[...] = o_ref[...] * (RET_QK_DIM ** -0.5)

    @pl.when(is_aq)
    def _():
        for c, part in enumerate(_head_rmsnorm(o_ref[...], gq_ref[...], HEAD_DIM ** -0.5)):
            o_ref[:, c * HEAD_DIM:(c + 1) * HEAD_DIM] = part

    @pl.when(is_ak)
    def _():
        for c, part in enumerate(_head_rmsnorm(o_ref[...], gk_ref[...], 1.0)):
            o_ref[:, c * HEAD_DIM:(c + 1) * HEAD_DIM] = part


def _in_proj(h, w, gq, gk, ret_q_w, ret_v_w, attn_w, tm, tn):
    n, d = h.shape
    in_w = w.shape[1]
    offs = (ret_q_w, 2 * ret_q_w + 2 * ret_v_w, 2 * ret_q_w + 2 * ret_v_w + attn_w)
    for o in offs + (ret_q_w, attn_w):
        assert o % tn == 0
    rk_tiles = (offs[0] // tn, (offs[0] + ret_q_w) // tn)
    aq_tiles = (offs[1] // tn, (offs[1] + attn_w) // tn)
    ak_tiles = (offs[2] // tn, (offs[2] + attn_w) // tn)
    body = functools.partial(_in_proj_body, rk_tiles=rk_tiles, aq_tiles=aq_tiles, ak_tiles=ak_tiles)
    return pl.pallas_call(
        body,
        grid=(n // tm, in_w // tn),
        in_specs=[
            pl.BlockSpec((tm, d), lambda i, j: (i, 0)),
            pl.BlockSpec((d, tn), lambda i, j: (0, j)),
            pl.BlockSpec((1, HEAD_DIM), lambda i, j: (0, 0)),
            pl.BlockSpec((1, HEAD_DIM), lambda i, j: (0, 0)),
        ],
        out_specs=pl.BlockSpec((tm, tn), lambda i, j: (i, j)),
        out_shape=jax.ShapeDtypeStruct((n, in_w), F32),
        compiler_params=_cparams(2, 56),
        name="in_proj",
    )(h, w, gq.reshape(1, HEAD_DIM), gk.reshape(1, HEAD_DIM))


def _ret_fwd_body(lgf_ref, lgb_ref, q_ref, k_ref, v_ref, o_ref, dmat_ref, state_ref, *, n_heads, start_chunks):
    c = pl.program_id(0)
    C = RET_CHUNK

    @pl.when(c == 0)
    def _():
        row = lax.broadcasted_iota(I32, (C, C), 0)
        col = lax.broadcasted_iota(I32, (C, C), 1)
        diff = (row - col).astype(F32)
        for h in range(n_heads):
            fwd = jnp.where(diff >= 0, jnp.exp(jnp.maximum(diff, 0.0) * lgf_ref[h]), 0.0)
            bwd = jnp.where(diff < 0, jnp.exp(jnp.maximum(-diff, 0.0) * lgb_ref[h]), 0.0)
            dmat_ref[h] = fwd + bwd

    @pl.when(_any_of(c, start_chunks))
    def _():
        state_ref[...] = jnp.zeros_like(state_ref)

    pos = lax.broadcasted_iota(I32, (C, 1), 0).astype(F32)
    for h in range(n_heads):
        lgf = lgf_ref[h]
        q = q_ref[:, h * RET_QK_DIM:(h + 1) * RET_QK_DIM].astype(BF16)
        k = k_ref[:, h * RET_QK_DIM:(h + 1) * RET_QK_DIM]
        v = v_ref[:, h * RET_V_DIM:(h + 1) * RET_V_DIM].astype(BF16)
        scores = _dot_nt(q, k.astype(BF16)) * dmat_ref[h]
        inner = _dot(scores.astype(BF16), v)
        st = state_ref[h]
        cross = _dot(q, st.astype(BF16)) * jnp.exp((pos + 1.0) * lgf)
        o_ref[:, h * RET_V_DIM:(h + 1) * RET_V_DIM] = inner + cross
        kw = k * jnp.exp((C - 1.0 - pos) * lgf)
        kv = _dot(kw.T.astype(BF16), v)
        state_ref[h] = st * jnp.exp(jnp.zeros((1, RET_V_DIM), F32) + C * lgf) + kv


def _ret_bwd_body(lgb_ref, q_ref, k_ref, v_ref, rg_ref, o1_ref, g_ref, o_ref, state_ref, *, n_heads, n_chunks, end_chunks):
    ci = n_chunks - 1 - pl.program_id(0)
    C = RET_CHUNK

    @pl.when(_any_of(ci, end_chunks))
    def _():
        state_ref[...] = jnp.zeros_like(state_ref)

    pos = lax.broadcasted_iota(I32, (C, 1), 0).astype(F32)
    for h in range(n_heads):
        lgb = lgb_ref[h]
        vs = slice(h * RET_V_DIM, (h + 1) * RET_V_DIM)
        q = q_ref[:, h * RET_QK_DIM:(h + 1) * RET_QK_DIM].astype(BF16)
        k = k_ref[:, h * RET_QK_DIM:(h + 1) * RET_QK_DIM]
        v = v_ref[:, vs].astype(BF16)
        st = state_ref[h]
        cross = _dot(q, st.astype(BF16)) * jnp.exp((C - pos) * lgb)
        o = o1_ref[:, vs] + cross
        ms = jnp.mean(o * o, axis=-1, keepdims=True)
        r = o * lax.rsqrt(ms + RMS_EPS) * g_ref[:, vs]
        gate = rg_ref[:, vs]
        r = r * (gate / (1.0 + jnp.exp(-gate)))
        o_ref[:, vs] = r.astype(o_ref.dtype)
        kw = k * jnp.exp(pos * lgb)
        kv = _dot(kw.T.astype(BF16), v)
        state_ref[h] = st * jnp.exp(jnp.zeros((1, RET_V_DIM), F32) + C * lgb) + kv


def _retention(proj, lg_f, lg_b, ret_norm_g, n_heads, seq_lens):
    n = proj.shape[0]
    C = RET_CHUNK
    qw = n_heads * RET_QK_DIM
    vw = n_heads * RET_V_DIM
    n_chunks = n // C
    starts, ends, acc = [], [], 0
    for t in seq_lens:
        starts.append(acc // C)
        acc += t
        ends.append(acc // C - 1)
    smem = pl.BlockSpec(memory_space=pltpu.SMEM)
    o1 = pl.pallas_call(
        functools.partial(_ret_fwd_body, n_heads=n_heads, start_chunks=tuple(starts)),
        grid=(n_chunks,),
        in_specs=[
            smem, smem,
            pl.BlockSpec((C, qw), lambda c: (c, 0)),
            pl.BlockSpec((C, qw), lambda c: (c, 1)),
            pl.BlockSpec((C, vw), lambda c: (c, 1)),
        ],
        out_specs=pl.BlockSpec((C, vw), lambda c: (c, 0)),
        out_shape=jax.ShapeDtypeStruct((n, vw), F32),
        scratch_shapes=[pltpu.VMEM((n_heads, C, C), F32), pltpu.VMEM((n_heads, RET_QK_DIM, RET_V_DIM), F32)],
        compiler_params=_cparams(1, 32),
        name="retention_fwd",
    )(lg_f, lg_b, proj, proj, proj)
    last = n_chunks - 1
    return pl.pallas_call(
        functools.partial(_ret_bwd_body, n_heads=n_heads, n_chunks=n_chunks, end_chunks=tuple(ends)),
        grid=(n_chunks,),
        in_specs=[
            smem,
            pl.BlockSpec((C, qw), lambda c: (last - c, 0)),
            pl.BlockSpec((C, qw), lambda c: (last - c, 1)),
            pl.BlockSpec((C, vw), lambda c: (last - c, 1)),
            pl.BlockSpec((C, vw), lambda c: (last - c, 2)),
            pl.BlockSpec((C, vw), lambda c: (last - c, 0)),
            pl.BlockSpec((1, vw), lambda c: (0, 0)),
        ],
        out_specs=pl.BlockSpec((C, vw), lambda c: (last - c, 0)),
        out_shape=jax.ShapeDtypeStruct((n, vw), BF16),
        scratch_shapes=[pltpu.VMEM((n_heads, RET_QK_DIM, RET_V_DIM), F32)],
        compiler_params=_cparams(1, 32),
        name="retention_bwd",
    )(lg_b, proj, proj, proj, proj, o1, ret_norm_g.reshape(1, vw))


def _deinterleave(x, dil):
    return x if dil == 1 else pltpu.einshape("(ir)c->(ri)c", x, r=dil)


def _interleave(x, dil):
    return x if dil == 1 else pltpu.einshape("(ri)c->(ir)c", x, r=dil)


def _attn_unit_rows(tq, dil):
    return min(256, tq // dil)


def _attn_body(slopes_ref, q_ref, k0_ref, kn_ref, v0_ref, vn_ref, o_ref, kbuf, vbuf, qbuf, *bias_refs,
               tq, start_tiles, end_tiles):
    h, i = pl.program_id(0), pl.program_id(1)
    rad = DIL_RADIUS
    unit_rows = sorted({_attn_unit_rows(tq, dil) for _, dil in DIL_BRANCHES})
    bias_of = dict(zip(unit_rows, bias_refs))

    @pl.when(jnp.logical_and(h == 0, i == 0))
    def _():
        for R, ref in bias_of.items():
            ii = lax.broadcasted_iota(I32, (R, R + 2 * rad), 0)
            jj = lax.broadcasted_iota(I32, (R, R + 2 * rad), 1)
            dist = jnp.abs(jj - rad - ii)
            ref[...] = jnp.where(dist <= rad, -dist.astype(F32), MASKED)

    def fill(slot, k_ref, v_ref):
        kx, vx = k_ref[...], v_ref[...]
        for bi, (_, dil) in enumerate(DIL_BRANCHES):
            kbuf[slot, bi] = _deinterleave(kx, dil).astype(BF16)
            vbuf[slot, bi] = _deinterleave(vx, dil).astype(BF16)

    @pl.when(i == 0)
    def _():
        fill(0, k0_ref, v0_ref)
        kbuf[2] = jnp.zeros_like(kbuf[2])
        vbuf[2] = jnp.zeros_like(vbuf[2])

    cur, nxt, prv = i % 3, (i + 1) % 3, (i + 2) % 3
    fill(nxt, kn_ref, vn_ref)
    qx = q_ref[...]
    for bi, (_, dil) in enumerate(DIL_BRANCHES):
        qbuf[bi] = _deinterleave(qx, dil).astype(BF16)

    slope = slopes_ref[h]
    no_prev = jnp.where(_any_of(i, start_tiles), 1.0, 0.0)
    no_next = jnp.where(_any_of(i, end_tiles), 1.0, 0.0)

    groups = {}
    for bi, (_, dil) in enumerate(DIL_BRANCHES):
        sub = tq // dil
        R = _attn_unit_rows(tq, dil)
        for residue in range(dil):
            for u in range(sub // R):
                groups.setdefault(R, []).append((bi, dil, residue * sub, sub, u * R, u == 0, u == sub // R - 1))

    def window(buf, bi, base, sub, off, first, final, R):
        parts = []
        if first:
            parts.append(buf[prv, bi, base + sub - rad:base + sub, :])
        parts.append(buf[cur, bi, base + off - (0 if first else rad):base + off + R + (0 if final else rad), :])
        if final:
            parts.append(buf[nxt, bi, base:base + rad, :])
        return jnp.concatenate(parts, axis=0) if len(parts) > 1 else parts[0]

    outs, lses = [None] * len(DIL_BRANCHES), [None] * len(DIL_BRANCHES)
    for R, units in groups.items():
        B, W = len(units), R + 2 * rad
        q = jnp.stack([qbuf[bi, base + off:base + off + R, :] for bi, _, base, _, off, _, _ in units])
        k = jnp.stack([window(kbuf, bi, base, sub, off, fi, fa, R) for bi, _, base, sub, off, fi, fa in units])
        v = jnp.stack([window(vbuf, bi, base, sub, off, fi, fa, R) for bi, _, base, sub, off, fi, fa in units])
        b_idx = lax.broadcasted_iota(I32, (B, 1, 1), 0)
        col = lax.broadcasted_iota(I32, (1, 1, W), 2)
        dil_b = jnp.zeros((B, 1, 1), F32)
        for dil in sorted({u[1] for u in units}):
            dil_b = dil_b + jnp.where(_any_of(b_idx, [b for b, u in enumerate(units) if u[1] == dil]), float(dil), 0.0)
        first_b = jnp.where(_any_of(b_idx, [b for b, u in enumerate(units) if u[5]]), no_prev, 0.0)
        final_b = jnp.where(_any_of(b_idx, [b for b, u in enumerate(units) if u[6]]), no_next, 0.0)
        s = lax.dot_general(q, k, (((2,), (2,)), ((0,), (0,))), preferred_element_type=F32)
        s = s + (slope * dil_b) * bias_of[R][...][None]
        s = s + first_b * jnp.where(col < rad, MASKED, 0.0)
        s = s + final_b * jnp.where(col >= R + rad, MASKED, 0.0)
        m = jnp.max(s, axis=-1, keepdims=True)
        p = jnp.exp(s - m)
        l = jnp.sum(p, axis=-1, keepdims=True)
        o = lax.dot_general(p.astype(BF16), v, (((2,), (1,)), ((0,), (0,))), preferred_element_type=F32) / l
        lse = jnp.broadcast_to(m + jnp.log(l), (B, R, LANES))
        for bi, (_, dil) in enumerate(DIL_BRANCHES):
            mine = [b for b, u in enumerate(units) if u[0] == bi]
            if mine:
                lo, hi = mine[0], mine[-1] + 1
                outs[bi] = _interleave(o[lo:hi].reshape((hi - lo) * R, HEAD_DIM), dil)
                lses[bi] = _interleave(lse[lo:hi].reshape((hi - lo) * R, LANES), dil)
    m = functools.reduce(jnp.maximum, lses)
    num = jnp.zeros((tq, LANES), F32)
    den = jnp.zeros((tq, LANES), F32)
    for o, lse in zip(outs, lses):
        w = jnp.exp(lse - m)
        num = num + w * o
        den = den + w
    o_ref[...] = num / den


def _dilated_attention(proj, slopes, n_heads, q_blk, seq_lens, tq):
    n = proj.shape[0]
    nt = n // tq
    starts, ends, acc = [], [], 0
    for t in seq_lens:
        assert t % tq == 0
        starts.append(acc // tq)
        acc += t
        ends.append(acc // tq - 1)
    qc, kc, vc = q_blk * n_heads, (q_blk + 1) * n_heads, (q_blk + 2) * n_heads
    blk = (tq, HEAD_DIM)
    nxt = lambda i: jnp.minimum(i + 1, nt - 1)
    n_br = len(DIL_BRANCHES)
    unit_rows = sorted({_attn_unit_rows(tq, dil) for _, dil in DIL_BRANCHES})
    body = functools.partial(_attn_body, tq=tq, start_tiles=tuple(starts), end_tiles=tuple(ends))
    return pl.pallas_call(
        body,
        grid=(n_heads, nt),
        in_specs=[
            pl.BlockSpec(memory_space=pltpu.SMEM),
            pl.BlockSpec(blk, lambda h, i: (i, qc + h)),
            pl.BlockSpec(blk, lambda h, i: (0, kc + h)),
            pl.BlockSpec(blk, lambda h, i: (nxt(i), kc + h)),
            pl.BlockSpec(blk, lambda h, i: (0, vc + h)),
            pl.BlockSpec(blk, lambda h, i: (nxt(i), vc + h)),
        ],
        out_specs=pl.BlockSpec(blk, lambda h, i: (i, h)),
        out_shape=jax.ShapeDtypeStruct((n, n_heads * HEAD_DIM), F32),
        scratch_shapes=[
            pltpu.VMEM((3, n_br, tq, HEAD_DIM), BF16),
            pltpu.VMEM((3, n_br, tq, HEAD_DIM), BF16),
            pltpu.VMEM((n_br, tq, HEAD_DIM), BF16),
        ] + [pltpu.VMEM((r, r + 2 * DIL_RADIUS), F32) for r in unit_rows],
        compiler_params=_cparams(2, 32),
        name="dilated_attention",
    )(slopes, proj, proj, proj, proj, proj)


def _out_proj_body(ret_ref, att_ref, g_ref, w_ref, xa_ref, xb_ref, oa_ref, ob_ref, cat_ref, *, ret_w, a_tiles):
    i = pl.program_id(0)

    @pl.when(pl.program_id(1) == 0)
    def _():
        cat_ref[:, :ret_w] = ret_ref[...]
        a = att_ref[...]
        ms = jnp.mean(a * a, axis=-1, keepdims=True)
        cat_ref[:, ret_w:] = (a * lax.rsqrt(ms + RMS_EPS) * g_ref[...]).astype(BF16)

    mixed = _dot(cat_ref[...], w_ref[...])

    @pl.when(i < a_tiles)
    def _():
        oa_ref[...] = xa_ref[...] + mixed

    @pl.when(i >= a_tiles)
    def _():
        ob_ref[...] = xb_ref[...] + mixed


def _out_proj(ret, att, attn_out_g, w_out, xa, xb, tm, tn):
    (na, d), nb = xa.shape, xb.shape[0]
    assert na % tm == 0 and nb % tm == 0
    a_tiles = na // tm
    last_j = d // tn - 1
    ret_w, attn_w = ret.shape[1], att.shape[1]
    in_a = lambda i, j: (jnp.minimum(i, a_tiles - 1), jnp.where(i < a_tiles, j, last_j))
    in_b = lambda i, j: (jnp.maximum(i - a_tiles, 0), jnp.where(i >= a_tiles, j, 0))
    return pl.pallas_call(
        functools.partial(_out_proj_body, ret_w=ret_w, a_tiles=a_tiles),
        grid=((na + nb) // tm, d // tn),
        in_specs=[
            pl.BlockSpec((tm, ret_w), lambda i, j: (i, 0)),
            pl.BlockSpec((tm, attn_w), lambda i, j: (i, 0)),
            pl.BlockSpec((1, attn_w), lambda i, j: (0, 0)),
            pl.BlockSpec((ret_w + attn_w, tn), lambda i, j: (0, j)),
            pl.BlockSpec((tm, tn), in_a),
            pl.BlockSpec((tm, tn), in_b),
        ],
        out_specs=[pl.BlockSpec((tm, tn), in_a), pl.BlockSpec((tm, tn), in_b)],
        out_shape=[jax.ShapeDtypeStruct((na, d), F32), jax.ShapeDtypeStruct((nb, d), F32)],
        scratch_shapes=[pltpu.VMEM((tm, ret_w + attn_w), BF16)],
        compiler_params=_cparams(2, 56),
        name="out_proj",
    )(ret, att, attn_out_g.reshape(1, attn_w), w_out, xa, xb)


def _ffn_rmsnorm(x, g_row):
    ms = jnp.mean(x * x, axis=-1, keepdims=True)
    return x * lax.rsqrt(ms + RMS_EPS) * g_row


def _dot_split(a, b):
    a_hi, b_hi = a.astype(BF16), b.astype(BF16)
    a_lo = (a - a_hi.astype(F32)).astype(BF16)
    b_lo = (b - b_hi.astype(F32)).astype(BF16)
    return _dot(a_hi, b_hi) + (_dot(a_hi, b_lo) + _dot(a_lo, b_hi))


def _ffn_norm_router_body(x_ref, g_ref, wr_ref, lg_ref):
    h = _ffn_rmsnorm(x_ref[...], g_ref[...])
    lg_ref[...] = _dot_split(h, wr_ref[...])


def _ffn_norm_router(x1, g, w_router, tm):
    n, d = x1.shape
    e = w_router.shape[1]
    return pl.pallas_call(
        _ffn_norm_router_body,
        grid=(n // tm,),
        in_specs=[
            pl.BlockSpec((tm, d), lambda i: (i, 0)),
            pl.BlockSpec((1, d), lambda i: (0, 0)),
            pl.BlockSpec((d, e), lambda i: (0, 0)),
        ],
        out_specs=pl.BlockSpec((tm, e), lambda i: (i, 0)),
        out_shape=jax.ShapeDtypeStruct((n, e), F32),
        compiler_params=_cparams(1, 48),
        name="ffn_norm_router",
    )(x1, g.reshape(1, d), w_router)


def _exclusive_cumsum_tokens(x, upper, lower):
    e, nb, _ = x.shape
    within = _dot(x.reshape(e * nb, LANES).astype(BF16), upper).reshape(e, nb, LANES)
    tot = jnp.sum(x, axis=2, keepdims=True)
    tot_b = jnp.concatenate([jnp.broadcast_to(tot[ei], (nb, LANES)) for ei in range(e)], axis=1)
    off_b = _dot(lower, tot_b.astype(BF16))
    off = jnp.stack([off_b[:, ei * LANES:(ei + 1) * LANES] for ei in range(e)], axis=0)
    return within + off, off


def _route_select_body(lt_ref, aff_ref, sel_ref, pos_ref, off_ref, *, cap):
    lt = lt_ref[...]
    e, nb, _ = lt.shape
    m = jnp.max(lt, axis=0, keepdims=True)
    ex = jnp.exp(lt - m)
    aff = ex / jnp.sum(ex, axis=0, keepdims=True)
    aff_ref[...] = aff
    bits = pltpu.bitcast(aff, I32)

    def count(mask):
        c = jnp.sum(jnp.where(mask, 1.0, 0.0), axis=2, keepdims=True)
        return jnp.sum(c, axis=1, keepdims=True)

    thr = jnp.zeros((e, 1, 1), I32)
    for b in range(30, -1, -1):
        cand = thr | (1 << b)
        thr = jnp.where(count(bits >= cand) >= cap, cand, thr)

    ii = lax.broadcasted_iota(I32, (LANES, LANES), 0)
    jj = lax.broadcasted_iota(I32, (LANES, LANES), 1)
    upper = jnp.where(ii < jj, 1.0, 0.0).astype(BF16)
    bi = lax.broadcasted_iota(I32, (nb, nb), 0)
    bj = lax.broadcasted_iota(I32, (nb, nb), 1)
    lower = jnp.where(bj < bi, 1.0, 0.0).astype(BF16)

    gt = bits > thr
    eq = bits == thr
    need = cap - count(gt)
    eq_rank, _ = _exclusive_cumsum_tokens(jnp.where(eq, 1.0, 0.0), upper, lower)
    sel = jnp.where(gt | (eq & (eq_rank < need)), 1.0, 0.0)
    pos, off = _exclusive_cumsum_tokens(sel, upper, lower)
    sel_ref[...] = sel
    pos_ref[...] = pos
    off_ref[...] = off.astype(I32)


def _route_compact_body(off_ref, sel_ref, pos_ref, aff_ref, idx_ref, gate_ref, acc_i, acc_g, *, nb, n_tiles):
    e = pl.program_id(0)
    acc_i[...] = jnp.zeros_like(acc_i)
    acc_g[...] = jnp.zeros_like(acc_g)
    lane = lax.broadcasted_iota(I32, (1, LANES), 1).astype(F32)
    slot = lax.broadcasted_iota(I32, (LANES, 1), 0).astype(F32)

    def block(b, carry):
        first = off_ref[e, b] // LANES
        pos = pos_ref[0, pl.ds(b, 1), :]
        chosen = sel_ref[0, pl.ds(b, 1), :] > 0.0
        aff = aff_ref[0, pl.ds(b, 1), :]
        tok = lane + lax.convert_element_type(b * LANES, F32)
        for dj in range(2):
            t = first + dj
            match = chosen & (pos == slot + lax.convert_element_type(t * LANES, F32))
            acc_i[t] = acc_i[t] + jnp.where(match, tok, 0.0)
            acc_g[t] = acc_g[t] + jnp.where(match, aff, 0.0)
        return carry

    lax.fori_loop(0, nb, block, 0)
    for t in range(n_tiles):
        rows = slice(t * LANES, (t + 1) * LANES)
        idx_ref[0, rows, :] = jnp.broadcast_to(jnp.sum(acc_i[t], axis=1, keepdims=True), (LANES, LANES))
        gate_ref[0, rows, :] = jnp.broadcast_to(jnp.sum(acc_g[t], axis=1, keepdims=True), (LANES, LANES))


def _route(logits, cap):
    n, e = logits.shape
    nb = n // LANES
    n_tiles = cap // LANES
    lt = logits.T.reshape(e, nb, LANES)
    shp = jax.ShapeDtypeStruct((e, nb, LANES), F32)
    aff, sel, pos, off = pl.pallas_call(
        functools.partial(_route_select_body, cap=cap),
        out_shape=[shp, shp, shp, jax.ShapeDtypeStruct((e, nb, LANES), I32)],
        compiler_params=pltpu.CompilerParams(vmem_limit_bytes=48 * 1024 * 1024),
        name="route_select",
    )(lt)
    blk = pl.BlockSpec((1, nb, LANES), lambda ei, off_ref: (ei, 0, 0))
    oblk = pl.BlockSpec((1, cap, LANES), lambda ei, off_ref: (ei, 0, 0))
    oshp = jax.ShapeDtypeStruct((e, cap, LANES), F32)
    idx_b, gate_b = pl.pallas_call(
        functools.partial(_route_compact_body, nb=nb, n_tiles=n_tiles),
        grid_spec=pltpu.PrefetchScalarGridSpec(
            num_scalar_prefetch=1,
            grid=(e,),
            in_specs=[blk, blk, blk],
            out_specs=[oblk, oblk],
            scratch_shapes=[pltpu.VMEM((n_tiles + 2, LANES, LANES), F32)] * 2,
        ),
        out_shape=[oshp, oshp],
        compiler_params=_cparams(1, 32),
        name="route_compact",
    )(off[:, :, 0], sel, pos, aff)
    return idx_b[:, :, 0].astype(I32), gate_b


def _group_rows(step, rows, chunks, a_chunks):
    return (step % chunks) < a_chunks


def _gather_body(idx_ref, xa_hbm, xb_hbm, g_ref, o_ref, buf, sem, *, rows, n_steps, chunks, a_chunks, a_rows):
    s = pl.program_id(0)
    slot = s % 2

    def issue(step, into):
        def copy_from(src_hbm, shift):
            def one(r, carry):
                src = src_hbm.at[pl.ds(idx_ref[step * rows + r] - shift, 1), :]
                pltpu.make_async_copy(src, buf.at[into, pl.ds(r, 1), :], sem.at[into]).start()
                return carry

            lax.fori_loop(0, rows, one, 0, unroll=8)

        in_a = _group_rows(step, rows, chunks, a_chunks)

        @pl.when(in_a)
        def _():
            copy_from(xa_hbm, 0)

        @pl.when(jnp.logical_not(in_a))
        def _():
            copy_from(xb_hbm, a_rows)

    @pl.when(s == 0)
    def _():
        issue(0, 0)

    @pl.when(s + 1 < n_steps)
    def _():
        issue(s + 1, 1 - slot)

    pltpu.make_async_copy(xa_hbm.at[pl.ds(0, rows), :], buf.at[slot], sem.at[slot]).wait()
    g_row = g_ref[...]
    for r0 in range(0, rows, GATHER_NORM_ROWS):
        part = slice(r0, r0 + GATHER_NORM_ROWS)
        o_ref[part, :] = _ffn_rmsnorm(buf[slot, part, :], g_row).astype(o_ref.dtype)


def _gather_norm_rows(xa, xb, g, idx_flat, rows, slots_per_expert, a_slots):
    (na, d), total = xa.shape, idx_flat.shape[0]
    n_steps = total // rows
    assert slots_per_expert % rows == 0 and a_slots % rows == 0
    body = functools.partial(_gather_body, rows=rows, n_steps=n_steps, chunks=slots_per_expert // rows,
                             a_chunks=a_slots // rows, a_rows=na)
    return pl.pallas_call(
        body,
        grid_spec=pltpu.PrefetchScalarGridSpec(
            num_scalar_prefetch=1,
            grid=(n_steps,),
            in_specs=[pl.BlockSpec(memory_space=pl.ANY), pl.BlockSpec(memory_space=pl.ANY),
                      pl.BlockSpec((1, d), lambda s, idx: (0, 0))],
            out_specs=pl.BlockSpec((rows, d), lambda s, idx: (s, 0)),
            scratch_shapes=[pltpu.VMEM((2, rows, d), F32), pltpu.SemaphoreType.DMA((2,))],
        ),
        out_shape=jax.ShapeDtypeStruct((total, d), BF16),
        compiler_params=_cparams(1, 32),
        name="gather_rows",
    )(idx_flat, xa, xb, g.reshape(1, d))


def _moe_body(x_ref, wg_ref, wu_ref, wd_ref, gate_ref, o_ref, *, nf):
    f = pl.program_id(2)

    @pl.when(f == 0)
    def _():
        o_ref[...] = jnp.zeros_like(o_ref)

    x = x_ref[...]
    g = _dot(x, wg_ref[...].astype(BF16))
    u = _dot(x, wu_ref[...].astype(BF16))
    hid = (g / (1.0 + jnp.exp(-g)) * u).astype(BF16)
    for c in range(o_ref.shape[1] // MOE_OUT_CHUNK):
        cols = slice(c * MOE_OUT_CHUNK, (c + 1) * MOE_OUT_CHUNK)
        o_ref[:, cols] += _dot(hid, wd_ref[:, cols].astype(BF16))

    @pl.when(f == nf - 1)
    def _():
        o_ref[...] = o_ref[...] * gate_ref[:, 0:1]


def _moe(xe, w_gate, w_up, w_down, gate_b, tm, tf):
    e, d, fdim = w_gate.shape
    total = xe.shape[0]
    tiles = total // e // tm
    nf = fdim // tf
    assert nf >= 2 and fdim % tf == 0
    row = lambda ei, s, f: (ei * tiles + s, 0)
    return pl.pallas_call(
        functools.partial(_moe_body, nf=nf),
        grid=(e, tiles, nf),
        in_specs=[
            pl.BlockSpec((tm, d), row),
            pl.BlockSpec((None, d, tf), lambda ei, s, f: (ei, 0, f)),
            pl.BlockSpec((None, d, tf), lambda ei, s, f: (ei, 0, f)),
            pl.BlockSpec((None, tf, d), lambda ei, s, f: (ei, f, 0)),
            pl.BlockSpec((tm, LANES), row),
        ],
        out_specs=pl.BlockSpec((tm, d), row, pipeline_mode=pl.Buffered(1)),
        out_shape=jax.ShapeDtypeStruct((total, d), F32),
        compiler_params=_cparams(3, 60),
        name="expert_swiglu",
    )(xe, w_gate, w_up, w_down, gate_b)


def _scatter_body(idx_ref, ye_ref, ya_in, yb_in, ya_hbm, yb_hbm, buf, sem_in, sem_out,
                  *, rows, n_steps, chunks, a_chunks, a_rows):
    del ya_in, yb_in
    s = pl.program_id(0)
    c = s % chunks
    slot = s % 3

    def per_group(step, fn):
        in_a = _group_rows(step, rows, chunks, a_chunks)

        @pl.when(in_a)
        def _():
            fn(ya_hbm, 0)

        @pl.when(jnp.logical_not(in_a))
        def _():
            fn(yb_hbm, a_rows)

    def fetch(step):
        into = step % 3

        def from_group(y_hbm, shift):
            def one(r, carry):
                src = y_hbm.at[pl.ds(idx_ref[step * rows + r] - shift, 1), :]
                pltpu.make_async_copy(src, buf.at[into, pl.ds(r, 1), :], sem_in.at[into]).start()
                return carry

            lax.fori_loop(0, rows, one, 0, unroll=8)

        per_group(step, from_group)

    def write(step):
        out_of = step % 3

        def to_group(y_hbm, shift):
            def one(r, carry):
                dst = y_hbm.at[pl.ds(idx_ref[step * rows + r] - shift, 1), :]
                pltpu.make_async_copy(buf.at[out_of, pl.ds(r, 1), :], dst, sem_out.at[out_of]).start()
                return carry

            lax.fori_loop(0, rows, one, 0, unroll=8)

        per_group(step, to_group)

    def wait_write(ring_slot):
        pltpu.make_async_copy(buf.at[ring_slot], ya_hbm.at[pl.ds(0, rows), :], sem_out.at[ring_slot]).wait()

    @pl.when(s == 0)
    def _():
        fetch(0)

    pltpu.make_async_copy(ya_hbm.at[pl.ds(0, rows), :], buf.at[slot], sem_in.at[slot]).wait()
    buf[slot] = buf[slot] + ye_ref[...]
    write(s)

    @pl.when(c == 0)
    def _():
        fetch(s + 1)
        fetch(s + 2)

    @pl.when(jnp.logical_and(c >= 1, c <= chunks - 3))
    def _():
        wait_write((s + 2) % 3)
        fetch(s + 2)

    @pl.when(c == chunks - 1)
    def _():
        wait_write((s + 1) % 3)
        wait_write((s + 2) % 3)
        wait_write(slot)

        @pl.when(s + 1 < n_steps)
        def _():
            fetch(s + 1)


def _scatter_add(ya, yb, ye, idx_flat, rows, slots_per_expert, a_slots):
    (na, d), nb = ya.shape, yb.shape[0]
    total = ye.shape[0]
    n_steps = total // rows
    chunks = slots_per_expert // rows
    assert chunks >= 3 and slots_per_expert % rows == 0 and a_slots % rows == 0
    body = functools.partial(_scatter_body, rows=rows, n_steps=n_steps, chunks=chunks,
                             a_chunks=a_slots // rows, a_rows=na)
    anyspace = pl.BlockSpec(memory_space=pl.ANY)
    return pl.pallas_call(
        body,
        grid_spec=pltpu.PrefetchScalarGridSpec(
            num_scalar_prefetch=1,
            grid=(n_steps,),
            in_specs=[pl.BlockSpec((rows, d), lambda s, idx: (s, 0)), anyspace, anyspace],
            out_specs=[anyspace, anyspace],
            scratch_shapes=[pltpu.VMEM((3, rows, d), F32), pltpu.SemaphoreType.DMA((3,)),
                            pltpu.SemaphoreType.DMA((3,))],
        ),
        out_shape=[jax.ShapeDtypeStruct((na, d), F32), jax.ShapeDtypeStruct((nb, d), F32)],
        input_output_aliases={2: 0, 3: 1},
        compiler_params=_cparams(1, 32),
        name="scatter_add",
    )(idx_flat, ye, ya, yb)


def _tile(n, pref):
    t = min(n, pref)
    assert n % t == 0
    return t


def _encoder_layer(xa, xb, seq_lens, attn_norm_g, w_in, ret_decay_fwd, ret_decay_bwd, ret_norm_g,
                   q_norm_g, k_norm_g, attn_out_g, w_out, ffn_norm_g, w_router, w_gate, w_up, w_down):
    group_tokens = (xa.shape[0], xb.shape[0])
    n, d = sum(group_tokens), xa.shape[1]
    n_ret = ret_decay_fwd.shape[0]
    ret_q_w, ret_v_w = n_ret * RET_QK_DIM, n_ret * RET_V_DIM
    attn_w = attn_out_g.shape[0]
    n_attn = attn_w // HEAD_DIM
    assert ret_v_w == attn_w and w_in.shape[1] == 2 * ret_q_w + 2 * ret_v_w + 3 * attn_w
    n_exp = w_router.shape[1]

    row_tile = math.gcd(512, *group_tokens)
    h = _rmsnorm_rows(xa, xb, attn_norm_g, BF16, row_tile)
    proj = _in_proj(h, w_in.astype(BF16), q_norm_g, k_norm_g, ret_q_w, ret_v_w, attn_w,
                    _tile(n, 1024), _tile(ret_q_w, 1024))
    lg_f = jnp.log1p(-jnp.exp2(-ret_decay_fwd.astype(F32)))
    lg_b = jnp.log1p(-jnp.exp2(-ret_decay_bwd.astype(F32)))
    ret = _retention(proj, lg_f, lg_b, ret_norm_g, n_ret, seq_lens)
    slopes = jnp.exp2(-8.0 * jnp.arange(1, n_attn + 1, dtype=F32) / n_attn)
    att = _dilated_attention(proj, slopes, n_attn, 3, seq_lens, 1024)
    x1a, x1b = _out_proj(ret, att, attn_out_g, w_out.astype(BF16), xa, xb, row_tile, _tile(d, 1024))

    idx_parts, gate_parts, start = [], [], 0
    for x1g in (x1a, x1b):
        tokens = x1g.shape[0]
        cap = EXPERT_CAPACITY * tokens // n_exp
        logits = _ffn_norm_router(x1g, ffn_norm_g, w_router, row_tile)
        idx_g, gate_g = _route(logits, cap)
        idx_parts.append(idx_g + start)
        gate_parts.append(gate_g)
        start += tokens
    idx = jnp.concatenate(idx_parts, axis=1)
    gate_b = jnp.concatenate(gate_parts, axis=1)
    slots, a_slots = idx.shape[1], idx_parts[0].shape[1]
    idx_flat = idx.reshape(n_exp * slots)
    tm = math.gcd(1024, *[EXPERT_CAPACITY * t // n_exp for t in group_tokens])
    rows = _tile(tm, 128)
    xe = _gather_norm_rows(x1a, x1b, ffn_norm_g, idx_flat, rows, slots, a_slots)
    ye = _moe(xe, w_gate, w_up, w_down, gate_b.reshape(n_exp * slots, LANES), tm, 256)
    return _scatter_add(x1a, x1b, ye, idx_flat, rows, slots, a_slots)


def kernel(x_prompt, x_sample, attn_norm_g, w_in, ret_decay_fwd, ret_decay_bwd, ret_norm_g, q_norm_g, k_norm_g,
           attn_out_g, w_out, ffn_norm_g, w_router, w_gate, w_up, w_down):
    depth = w_in.shape[0]
    d = x_prompt.shape[-1]
    bp, tp, _ = x_prompt.shape
    bs, ts, _ = x_sample.shape
    n_p, n_s = bp * tp, bs * ts
    xa, xb = x_prompt.reshape(n_p, d), x_sample.reshape(n_s, d)
    seq_lens = (tp,) * bp + (ts,) * bs
    for l in range(depth):
        xa, xb = _encoder_layer(
            xa, xb, seq_lens, attn_norm_g[l], w_in[l], ret_decay_fwd[l], ret_decay_bwd[l], ret_norm_g[l],
            q_norm_g[l], k_norm_g[l], attn_out_g[l], w_out[l], ffn_norm_g[l], w_router[l],
            w_gate[l], w_up[l], w_down[l])
    return xa.reshape(bp, tp, d), xb.reshape(bs, ts, d)
```
